```python
import math
import jax, jax.numpy as jnp
from jax import lax
import numpy as np

D_MODEL = 1024
BATCH = 4
SEQ = 8192
DEPTH = 1
DEC_BATCH = 32
DEC_SEQ = 1
PAST_LEN = 16384
PAGE_SIZE = 128

RW_HEADS = 8
RW_HEAD = 64
RW_WIDTH = RW_HEADS * RW_HEAD
DECAY_LORA = 64
AAA_LORA = 64
GATE_LORA = 128
RW_COLS = 3 * RW_WIDTH + DECAY_LORA + AAA_LORA + GATE_LORA
RW_GN_EPS = 1e-5 * RW_HEAD * RW_HEAD
DA_HEADS = 4
DA_QK = 64
DA_V = 2 * DA_QK
DA_QK_WIDTH = 2 * DA_HEADS * DA_QK
DA_V_WIDTH = DA_HEADS * DA_V
DA_COLS = 2 * DA_QK_WIDTH + DA_V_WIDTH
GATE_COLS = 2 * D_MODEL
IN_COLS = RW_COLS + DA_COLS + GATE_COLS
D_FF = 128 * ((8 * D_MODEL // 3 + 127) // 128)
ROPE_THETA = 10000.0
Q_BLOCK = 128
NORM_EPS = 1e-6
F32 = jnp.float32

kernel_name = "rwkv7_diffattn_macaron_hybrid_step"


def rms_norm(x, g):
    xf = x.astype(F32)
    y = xf * lax.rsqrt(jnp.mean(jnp.square(xf), axis=-1, keepdims=True) + NORM_EPS)
    return (y * g.astype(F32)).astype(x.dtype)


def swiglu(x, w_up, w_down):
    gate, up = jnp.split(x @ w_up, 2, axis=-1)
    return (jax.nn.silu(gate) * up) @ w_down


def rope(x, pos):
    half = x.shape[-1] // 2
    inv = ROPE_THETA ** (-jnp.arange(half, dtype=F32) / half)
    ang = pos.astype(F32)[:, None] * inv[None, :]
    cos = jnp.cos(ang)[:, None, :]
    sin = jnp.sin(ang)[:, None, :]
    xf = x.astype(F32)
    x1, x2 = xf[..., :half], xf[..., half:]
    return jnp.concatenate([x1 * cos - x2 * sin, x2 * cos + x1 * sin], axis=-1).astype(x.dtype)


def diff_attn_core(q, k, v, q_pos, k_pos, lam):
    b, tq = q.shape[:2]
    tk = k.shape[1]
    s = jnp.einsum("bqhd,bkhd->bhqk", q, k).astype(F32) * (DA_QK ** -0.5)
    s = jnp.where(k_pos[None, :] <= q_pos[:, None], s, -jnp.inf)
    p = jax.nn.softmax(s, axis=-1).reshape(b, DA_HEADS, 2, tq, tk)
    w = p[:, :, 0] - lam * p[:, :, 1]
    return jnp.einsum("bhqk,bkhd->bqhd", w.astype(v.dtype), v)


def diff_attn_prompt(q, k, v, lam):
    b, s = q.shape[:2]
    nb = s // Q_BLOCK
    pos = jnp.arange(s, dtype=jnp.int32)
    qb = q.reshape(b, nb, Q_BLOCK, 2 * DA_HEADS, DA_QK).swapaxes(0, 1)
    pb = pos.reshape(nb, Q_BLOCK)
    out = lax.map(lambda a: diff_attn_core(a[0], k, v, a[1], pos, lam), (qb, pb))
    return out.swapaxes(0, 1).reshape(b, s, DA_HEADS, DA_V)


def diff_attn_sample(q, k_new, v_new, cache_k, cache_v, page_table, lam):
    t = q.shape[1]
    n_past = page_table.shape[1] * PAGE_SIZE
    q_pos = n_past + jnp.arange(t, dtype=jnp.int32)
    k_pos = jnp.arange(n_past + t, dtype=jnp.int32)

    def one(args):
        qi, ki, vi, pt = args
        kp = cache_k[pt].reshape(n_past, 2 * DA_HEADS, DA_QK)
        vp = cache_v[pt].reshape(n_past, DA_HEADS, DA_V)
        kk = jnp.concatenate([kp, ki.astype(kp.dtype)], axis=0)
        vv = jnp.concatenate([vp, vi.astype(vp.dtype)], axis=0)
        return diff_attn_core(qi[None], kk[None], vv[None], q_pos, k_pos, lam)[0].astype(qi.dtype)

    return lax.map(one, (q, k_new, v_new, page_table))


def wkv7_scan(state0, r, w, k, v, a, b):
    def step(S, inp):
        r_t, w_t, k_t, v_t, a_t, b_t = inp
        Sa = jnp.einsum("bhij,bhj->bhi", S, a_t)
        S = S * w_t[:, :, None, :] + Sa[..., None] * b_t[:, :, None, :] + v_t[..., None] * k_t[:, :, None, :]
        return S, jnp.einsum("bhij,bhj->bhi", S, r_t)

    xs = tuple(t.swapaxes(0, 1) for t in (r, w, k, v, a, b))
    S, ys = lax.scan(step, state0, xs)
    return ys.swapaxes(0, 1), S


def rwkv7_branch(z, z_last, wkv0, lp):
    bsz, t, _ = z.shape
    z_prev = jnp.concatenate([z_last[:, None, :].astype(z.dtype), z[:, :-1]], axis=1)
    zm = z + (z_prev - z) * lp["rw_mu"]
    o1 = RW_WIDTH
    o4 = 3 * RW_WIDTH + DECAY_LORA
    r, k, v, wd, ad, gd = jnp.split(zm, [o1, 2 * o1, 3 * o1, o4, o4 + AAA_LORA], axis=-1)
    w_log = -jax.nn.softplus(-(lp["rw_w0"] + jnp.tanh(wd) @ lp["rw_w2"]).astype(F32)) - 0.5
    decay = jnp.exp(-jnp.exp(w_log))
    a = jax.nn.sigmoid(lp["rw_a0"] + ad @ lp["rw_a2"])
    g = jax.nn.sigmoid(gd) @ lp["rw_g2"]
    heads = lambda u: u.reshape(bsz, t, RW_HEADS, RW_HEAD).astype(F32)
    kk = heads(k * lp["rw_k_k"])
    kk = kk / jnp.maximum(jnp.linalg.norm(kk, axis=-1, keepdims=True), 1e-12)
    k = k * (1.0 + (a - 1.0) * lp["rw_k_a"])
    rh, kh, vh, ah = heads(r), heads(k), heads(v), heads(a)
    y, wkv = wkv7_scan(wkv0.astype(F32), rh, heads(decay), kh, vh, -kk, kk * ah)
    mu = jnp.mean(y, axis=-1, keepdims=True)
    var = jnp.mean(jnp.square(y - mu), axis=-1, keepdims=True)
    y = (y - mu) * lax.rsqrt(var + RW_GN_EPS)
    y = y * lp["rw_ln_w"].astype(F32).reshape(RW_HEADS, RW_HEAD) + lp["rw_ln_b"].astype(F32).reshape(RW_HEADS, RW_HEAD)
    y = y + jnp.sum(rh * kh * lp["rw_r_k"].astype(F32), axis=-1, keepdims=True) * vh
    y = y.reshape(bsz, t, RW_WIDTH).astype(z.dtype) * g
    return y, wkv.astype(wkv0.dtype), z[:, -1]


def decoder_layer(x, pos, wkv0, z_last, attend, lp, lam_init):
    bsz, t, _ = x.shape
    x = x + 0.5 * swiglu(rms_norm(x, lp["ffn1_norm"]), lp["ffn1_up"], lp["ffn1_down"])
    h = rms_norm(x, lp["mix_norm"])
    proj = h @ lp["w_in"]
    z_rw, q, k, v, gates = jnp.split(
        proj, [RW_COLS, RW_COLS + DA_QK_WIDTH, RW_COLS + 2 * DA_QK_WIDTH, RW_COLS + DA_COLS], axis=-1)
    y_a, wkv, z_new = rwkv7_branch(z_rw, z_last, wkv0, lp)
    q = rope(q.reshape(bsz, t, 2 * DA_HEADS, DA_QK), pos)
    k = rope(k.reshape(bsz, t, 2 * DA_HEADS, DA_QK), pos)
    v = v.reshape(bsz, t, DA_HEADS, DA_V)
    o = attend(q, k, v)
    o = rms_norm(o, lp["da_subln"].reshape(DA_HEADS, DA_V)) * (1.0 - lam_init)
    y_b = o.reshape(bsz, t, DA_V_WIDTH)
    g_a, g_b = jnp.split(jax.nn.sigmoid(gates + lp["gate_bias"]), 2, axis=-1)
    merged = g_a * (y_a @ lp["w_proj_a"]) + g_b * (y_b @ lp["w_proj_b"])
    x = x + merged @ lp["w_out"]
    x = x + 0.5 * swiglu(rms_norm(x, lp["ffn2_norm"]), lp["ffn2_up"], lp["ffn2_down"])
    return x, k, v, wkv, z_new


def setup_inputs(seed: int = 0) -> dict:
    key = jax.random.key(seed)
    ks = iter(jax.random.split(key, 64))
    nrm = lambda shape, s=1.0: jax.random.normal(next(ks), shape, F32) * s
    L = DEPTH
    n_pages = PAST_LEN // PAGE_SIZE
    n_phys = (DEC_BATCH * n_pages * 5) // 4
    perm = jax.random.permutation(next(ks), n_phys)
    page_table = perm[: DEC_BATCH * n_pages].reshape(DEC_BATCH, n_pages).astype(jnp.int32)
    return {
        "x_prompt": nrm((BATCH, SEQ, D_MODEL)),
        "x_sample": nrm((DEC_BATCH, DEC_SEQ, D_MODEL)),
        "cache_k": nrm((L, n_phys, PAGE_SIZE, 2 * DA_HEADS, DA_QK)),
        "cache_v": nrm((L, n_phys, PAGE_SIZE, DA_HEADS, DA_V)),
        "state_wkv": nrm((L, DEC_BATCH, RW_HEADS, RW_HEAD, RW_HEAD), 0.5),
        "state_shift": nrm((L, DEC_BATCH, RW_COLS)),
        "page_table": page_table,
        "ffn1_norm": 1.0 + nrm((L, D_MODEL), 0.05),
        "ffn1_up": nrm((L, D_MODEL, 2 * D_FF), D_MODEL ** -0.5),
        "ffn1_down": nrm((L, D_FF, D_MODEL), D_FF ** -0.5),
        "mix_norm": 1.0 + nrm((L, D_MODEL), 0.05),
        "w_in": nrm((L, D_MODEL, IN_COLS), D_MODEL ** -0.5),
        "gate_bias": nrm((L, GATE_COLS), 0.01),
        "rw_mu": jax.random.uniform(next(ks), (L, RW_COLS), F32),
        "rw_w0": -0.5 + nrm((L, RW_WIDTH), 0.3),
        "rw_w2": nrm((L, DECAY_LORA, RW_WIDTH), 0.1 * DECAY_LORA ** -0.5),
        "rw_a0": nrm((L, RW_WIDTH), 0.1),
        "rw_a2": nrm((L, AAA_LORA, RW_WIDTH), 0.1 * AAA_LORA ** -0.5),
        "rw_g2": nrm((L, GATE_LORA, RW_WIDTH), GATE_LORA ** -0.5),
        "rw_k_k": 0.85 + nrm((L, RW_WIDTH), 0.05),
        "rw_k_a": 1.0 + nrm((L, RW_WIDTH), 0.05),
        "rw_r_k": nrm((L, RW_HEADS, RW_HEAD), 0.1),
        "rw_ln_w": 1.0 + nrm((L, RW_WIDTH), 0.05),
        "rw_ln_b": nrm((L, RW_WIDTH), 0.01),
        "da_lq1": nrm((L, DA_QK), 0.1),
        "da_lk1": nrm((L, DA_QK), 0.1),
        "da_lq2": nrm((L, DA_QK), 0.1),
        "da_lk2": nrm((L, DA_QK), 0.1),
        "da_subln": 1.0 + nrm((L, DA_V_WIDTH), 0.05),
        "w_proj_a": nrm((L, RW_WIDTH, D_MODEL), RW_WIDTH ** -0.5),
        "w_proj_b": nrm((L, DA_V_WIDTH, D_MODEL), DA_V_WIDTH ** -0.5),
        "w_out": nrm((L, D_MODEL, D_MODEL), D_MODEL ** -0.5),
        "ffn2_norm": 1.0 + nrm((L, D_MODEL), 0.05),
        "ffn2_up": nrm((L, D_MODEL, 2 * D_FF), D_MODEL ** -0.5),
        "ffn2_down": nrm((L, D_FF, D_MODEL), D_FF ** -0.5),
        "final_norm": 1.0 + nrm((D_MODEL,), 0.05),
    }


def reference(x_prompt, x_sample, cache_k, cache_v, state_wkv, state_shift, page_table,
              ffn1_norm, ffn1_up, ffn1_down, mix_norm, w_in, gate_bias,
              rw_mu, rw_w0, rw_w2, rw_a0, rw_a2, rw_g2, rw_k_k, rw_k_a, rw_r_k, rw_ln_w, rw_ln_b,
              da_lq1, da_lk1, da_lq2, da_lk2, da_subln, w_proj_a, w_proj_b, w_out,
              ffn2_norm, ffn2_up, ffn2_down, final_norm):
    bp, sp = x_prompt.shape[:2]
    bs, ts = x_sample.shape[:2]
    n_past = page_table.shape[1] * PAGE_SIZE
    pos_p = jnp.arange(sp, dtype=jnp.int32)
    pos_s = n_past + jnp.arange(ts, dtype=jnp.int32)
    hp, hs = x_prompt, x_sample
    kp_l, vp_l, wp_l, zp_l = [], [], [], []
    ks_l, vs_l, ws_l, zs_l = [], [], [], []
    for l in range(DEPTH):
        lp = dict(ffn1_norm=ffn1_norm[l], ffn1_up=ffn1_up[l], ffn1_down=ffn1_down[l],
                  mix_norm=mix_norm[l], w_in=w_in[l], gate_bias=gate_bias[l],
                  rw_mu=rw_mu[l], rw_w0=rw_w0[l], rw_w2=rw_w2[l], rw_a0=rw_a0[l], rw_a2=rw_a2[l],
                  rw_g2=rw_g2[l], rw_k_k=rw_k_k[l], rw_k_a=rw_k_a[l], rw_r_k=rw_r_k[l],
                  rw_ln_w=rw_ln_w[l], rw_ln_b=rw_ln_b[l], da_subln=da_subln[l],
                  w_proj_a=w_proj_a[l], w_proj_b=w_proj_b[l], w_out=w_out[l],
                  ffn2_norm=ffn2_norm[l], ffn2_up=ffn2_up[l], ffn2_down=ffn2_down[l])
        lam_init = 0.8 - 0.6 * math.exp(-0.3 * l)
        lam = (jnp.exp(jnp.sum(da_lq1[l].astype(F32) * da_lk1[l].astype(F32)))
               - jnp.exp(jnp.sum(da_lq2[l].astype(F32) * da_lk2[l].astype(F32))) + lam_init)
        ck, cv = cache_k[l], cache_v[l]
        attend_p = lambda q, k, v, lam=lam: diff_attn_prompt(q, k, v, lam)
        attend_s = lambda q, k, v, lam=lam, ck=ck, cv=cv: diff_attn_sample(q, k, v, ck, cv, page_table, lam)
        hp, kp, vp, wp, zp = decoder_layer(
            hp, pos_p, jnp.zeros((bp, RW_HEADS, RW_HEAD, RW_HEAD), hp.dtype),
            jnp.zeros((bp, RW_COLS), hp.dtype), attend_p, lp, lam_init)
        hs, k_s, v_s, w_s, z_s = decoder_layer(
            hs, pos_s, state_wkv[l], state_shift[l], attend_s, lp, lam_init)
        kp_l.append(kp); vp_l.append(vp); wp_l.append(wp); zp_l.append(zp)
        ks_l.append(k_s); vs_l.append(v_s); ws_l.append(w_s); zs_l.append(z_s)
    y_prompt = rms_norm(hp, final_norm)
    y_sample = rms_norm(hs, final_norm)
    k_prompt, v_prompt = jnp.stack(kp_l), jnp.stack(vp_l)
    wkv_prompt, shift_prompt = jnp.stack(wp_l), jnp.stack(zp_l)
    k_sample, v_sample = jnp.stack(ks_l), jnp.stack(vs_l)
    wkv_sample, shift_sample = jnp.stack(ws_l), jnp.stack(zs_l)
    return (y_prompt, y_sample, k_prompt, v_prompt, wkv_prompt, shift_prompt,
            k_sample, v_sample, wkv_sample, shift_sample)
```

```python
import functools
import math

import jax
import jax.numpy as jnp
from jax import lax
from jax.experimental import pallas as pl
from jax.experimental.pallas import tpu as pltpu

F32 = jnp.float32
BF16 = jnp.bfloat16

NORM_EPS = 1e-6
ROPE_THETA = 10000.0
PAGE_SIZE = 128

RW_HEADS = 8
RW_HEAD = 64
RW_WIDTH = RW_HEADS * RW_HEAD
DECAY_LORA = 64
AAA_LORA = 64
GATE_LORA = 128
RW_COLS = 3 * RW_WIDTH + DECAY_LORA + AAA_LORA + GATE_LORA
RW_GN_EPS = 1e-5 * RW_HEAD * RW_HEAD
DA_HEADS = 4
DA_QK = 64
DA_V = 2 * DA_QK
DA_QK_WIDTH = 2 * DA_HEADS * DA_QK
DA_V_WIDTH = DA_HEADS * DA_V

LANES = 128
SCAN_LANE_BATCH = 4
SCAN_ILO = LANES // (SCAN_LANE_BATCH * RW_HEADS)
SCAN_IHI = RW_HEAD // SCAN_ILO
VMEM_LIMIT = 56 * 1024 * 1024


def _const_spec(shape):
    nd = len(shape)
    return pl.BlockSpec(shape, lambda *_: (0,) * nd, pipeline_mode=pl.Buffered(1))


def _params(sem):
    return pltpu.CompilerParams(dimension_semantics=sem, vmem_limit_bytes=VMEM_LIMIT)


def _rms(x, g):
    return x * lax.rsqrt(jnp.mean(x * x, axis=-1, keepdims=True) + NORM_EPS) * g


def _dot(a, b):
    return jnp.dot(a, b, preferred_element_type=F32)


def _seg_sum(x, ones_bf):
    hi = x.astype(BF16)
    lo = (x - hi.astype(F32)).astype(BF16)
    return _dot(hi, ones_bf) + _dot(lo, ones_bf)


def _ffn_kernel(x_ref, g_ref, wg_ref, wu_ref, wd_ref, *rest, final):
    if final:
        gf_ref, o_ref, h_scr, acc_scr = rest
    else:
        o_ref, h_scr, acc_scr = rest
    x = x_ref[...]
    h_scr[...] = _rms(x, g_ref[...]).astype(BF16)
    acc_scr[...] = jnp.zeros_like(acc_scr)

    def body(c, carry):
        hb = h_scr[...]
        gate = _dot(hb, wg_ref[c])
        up = _dot(hb, wu_ref[c])
        act = (gate * jax.nn.sigmoid(gate) * up).astype(BF16)
        acc_scr[...] += _dot(act, wd_ref[c])
        return carry

    lax.fori_loop(0, wg_ref.shape[0], body, 0)
    y = x_ref[...] + 0.5 * acc_scr[...]
    if final:
        y = _rms(y, gf_ref[...])
    o_ref[...] = y


def _ffn(x, norm_g, wg, wu, wd, final_g=None, tm=512):
    n, d = x.shape
    nc, _, tf = wg.shape
    tm = min(tm, n)
    final = final_g is not None
    in_specs = [pl.BlockSpec((tm, d), lambda i: (i, 0)), _const_spec((1, d)),
                _const_spec((nc, d, tf)), _const_spec((nc, d, tf)), _const_spec((nc, tf, d))]
    args = [x, norm_g, wg, wu, wd]
    if final:
        in_specs.append(_const_spec((1, d)))
        args.append(final_g)
    return pl.pallas_call(
        functools.partial(_ffn_kernel, final=final),
        grid=(n // tm,),
        in_specs=in_specs,
        out_specs=pl.BlockSpec((tm, d), lambda i: (i, 0)),
        out_shape=jax.ShapeDtypeStruct((n, d), F32),
        scratch_shapes=[pltpu.VMEM((tm, d), BF16), pltpu.VMEM((tm, d), F32)],
        compiler_params=_params(("parallel",)),
        name="ffn_final" if final else "ffn",
    )(*args)


def _mixproj_kernel(x_ref, g_ref, w_ref, gb_ref, cos_ref, sin_ref,
                    z_ref, q_ref, k_ref, v_ref, kb_ref, vb_ref, gate_ref):
    hb = _rms(x_ref[...], g_ref[...]).astype(BF16)
    for c in range(0, RW_COLS, 256):
        z_ref[:, c:c + 256] = _dot(hb, w_ref[:, c:c + 256])

    cos = cos_ref[...]
    sin = sin_ref[...]
    lane = lax.broadcasted_iota(jnp.int32, cos.shape, 1)
    first_half = (lane & (DA_QK // 2)) == 0

    def rope(xx):
        swapped = jnp.where(first_half, pltpu.roll(xx, LANES - DA_QK // 2, 1), pltpu.roll(xx, DA_QK // 2, 1))
        return xx * cos + swapped * sin

    q_off = RW_COLS
    k_off = q_off + DA_QK_WIDTH
    v_off = k_off + DA_QK_WIDTH
    g_off = v_off + DA_V_WIDTH
    q = _dot(hb, w_ref[:, q_off:q_off + DA_QK_WIDTH])
    k = _dot(hb, w_ref[:, k_off:k_off + DA_QK_WIDTH])
    for c in range(0, DA_QK_WIDTH, LANES):
        q_ref[:, c:c + LANES] = (rope(q[:, c:c + LANES]) * (DA_QK ** -0.5)).astype(BF16)
        kr = rope(k[:, c:c + LANES])
        k_ref[:, c:c + LANES] = kr
        kb_ref[:, c:c + LANES] = kr.astype(BF16)
    v = _dot(hb, w_ref[:, v_off:v_off + DA_V_WIDTH])
    v_ref[...] = v
    vb_ref[...] = v.astype(BF16)
    for c in range(0, gate_ref.shape[1], 512):
        gate_ref[:, c:c + 512] = jax.nn.sigmoid(
            _dot(hb, w_ref[:, g_off + c:g_off + c + 512]) + gb_ref[:, c:c + 512])


def _mixproj(x, norm_g, w_in, gate_bias, cos_t, sin_t, rope_blocks, tm=256):
    n, d = x.shape
    in_cols = w_in.shape[1]
    gate_cols = gate_bias.shape[1]
    tm = min(tm, n)
    row = lambda i: (i, 0)
    rope_map = lambda i: (i % rope_blocks, 0)
    outs = [(RW_COLS, F32), (DA_QK_WIDTH, BF16), (DA_QK_WIDTH, F32), (DA_V_WIDTH, F32),
            (DA_QK_WIDTH, BF16), (DA_V_WIDTH, BF16), (gate_cols, F32)]
    return pl.pallas_call(
        _mixproj_kernel,
        grid=(n // tm,),
        in_specs=[pl.BlockSpec((tm, d), row), _const_spec((1, d)), _const_spec((d, in_cols)),
                  _const_spec((1, gate_cols)),
                  pl.BlockSpec((tm, LANES), rope_map), pl.BlockSpec((tm, LANES), rope_map)],
        out_specs=[pl.BlockSpec((tm, w), row) for w, _ in outs],
        out_shape=[jax.ShapeDtypeStruct((n, w), dt) for w, dt in outs],
        compiler_params=_params(("parallel",)),
        name="mixproj",
    )(x, norm_g, w_in, gate_bias, cos_t, sin_t)


def _rwprep_kernel(z_ref, zp_ref, mu_ref, w0_ref, a0_ref, kk_ref, ka_ref, rk_ref,
                   w2_ref, a2_ref, g2_ref, ones_ref,
                   r_ref, dec_ref, k_ref, v_ref, an_ref, bn_ref, g_ref, bonus_ref):
    z = z_ref[...]
    zm = z + (zp_ref[...] - z) * mu_ref[...]
    o1 = RW_WIDTH
    r = zm[:, :o1]
    k = zm[:, o1:2 * o1]
    v = zm[:, 2 * o1:3 * o1]
    lora_in = zm[:, 3 * o1:3 * o1 + LANES]
    gd = zm[:, 3 * o1 + LANES:]
    ones = ones_ref[...]

    x = w0_ref[...] + _dot(jnp.tanh(lora_in).astype(BF16), w2_ref[...])
    neg = -x
    softplus = jnp.maximum(neg, 0.0) + jnp.log1p(jnp.exp(-jnp.abs(neg)))
    dec_ref[...] = jnp.exp(-jnp.exp(-softplus - 0.5))
    a = jax.nn.sigmoid(a0_ref[...] + _dot(lora_in.astype(BF16), a2_ref[...]))
    g_ref[...] = _dot(jax.nn.sigmoid(gd).astype(BF16), g2_ref[...])

    kk = k * kk_ref[...]
    norm = jnp.sqrt(_seg_sum(kk * kk, ones))
    kk = kk / jnp.maximum(norm, 1e-12)
    kmod = k * (1.0 + (a - 1.0) * ka_ref[...])
    r_ref[...] = r
    k_ref[...] = kmod
    v_ref[...] = v
    an_ref[...] = -kk
    bn_ref[...] = kk * a
    bonus_ref[...] = _seg_sum(r * kmod * rk_ref[...], ones) * v


def _rwprep(z, zprev, p, tm=256):
    n = z.shape[0]
    tm = min(tm, n)
    row = lambda i: (i, 0)
    vec = _const_spec((1, RW_WIDTH))
    return pl.pallas_call(
        _rwprep_kernel,
        grid=(n // tm,),
        in_specs=[pl.BlockSpec((tm, RW_COLS), row), pl.BlockSpec((tm, RW_COLS), row),
                  _const_spec((1, RW_COLS)), vec, vec, vec, vec, vec,
                  _const_spec((LANES, RW_WIDTH)), _const_spec((LANES, RW_WIDTH)),
                  _const_spec((GATE_LORA, RW_WIDTH)), _const_spec((RW_WIDTH, RW_WIDTH))],
        out_specs=[pl.BlockSpec((tm, RW_WIDTH), row)] * 8,
        out_shape=[jax.ShapeDtypeStruct((n, RW_WIDTH), F32)] * 8,
        compiler_params=_params(("parallel",)),
        name="rwprep",
    )(z, zprev, p["mu"], p["w0"], p["a0"], p["k_k"], p["k_a"], p["r_k"],
      p["w2"], p["a2"], p["g2"], p["ones"])


def _scan_kernel(a_ref, w_ref, b_ref, k_ref, r_ref, v_ref, s0_ref, y_ref, sf_ref, s_scr):
    tblk = pl.program_id(1)

    @pl.when(tblk == 0)
    def _():
        s_scr[...] = s0_ref[0]

    def step(t, carry):
        a = a_ref[0, t]
        w = w_ref[0, t]
        b = b_ref[0, t]
        k = k_ref[0, t]
        r = r_ref[0, t]
        v = v_ref[0, t]
        rows = []
        for ih in range(SCAN_IHI):
            s = s_scr[ih]
            sa = jnp.sum(s * a, axis=0, keepdims=True)
            sn = s * w + sa * b + v[ih:ih + 1, :] * k
            s_scr[ih] = sn
            rows.append(jnp.sum(sn * r, axis=0, keepdims=True))
        y_ref[0, t] = jnp.concatenate(rows, axis=0)
        return carry

    lax.fori_loop(0, a_ref.shape[1], step, 0)

    @pl.when(tblk == pl.num_programs(1) - 1)
    def _():
        sf_ref[0] = s_scr[...]


def _wkv_scan(a, w, b, k, r, v, s0, tb=32):
    n_p, t = a.shape[:2]
    tb = min(tb, t)
    keyspec = pl.BlockSpec((1, tb, RW_HEAD, LANES), lambda p, i: (p, i, 0, 0))
    valspec = pl.BlockSpec((1, tb, SCAN_IHI, LANES), lambda p, i: (p, i, 0, 0))
    stspec = pl.BlockSpec((1, SCAN_IHI, RW_HEAD, LANES), lambda p, i: (p, 0, 0, 0))
    return pl.pallas_call(
        _scan_kernel,
        grid=(n_p, t // tb),
        in_specs=[keyspec] * 5 + [valspec, stspec],
        out_specs=[valspec, stspec],
        out_shape=[jax.ShapeDtypeStruct((n_p, t, SCAN_IHI, LANES), F32),
                   jax.ShapeDtypeStruct((n_p, SCAN_IHI, RW_HEAD, LANES), F32)],
        scratch_shapes=[pltpu.VMEM((SCAN_IHI, RW_HEAD, LANES), F32)],
        compiler_params=_params(("parallel", "arbitrary")),
        name="wkv_scan",
    )(a, w, b, k, r, v, s0)


def _to_key_lanes(x, bsz, t):
    g = bsz // SCAN_LANE_BATCH
    x = x.reshape(g, SCAN_LANE_BATCH, t, RW_HEADS, RW_HEAD).transpose(0, 2, 4, 1, 3)
    x = jnp.broadcast_to(x[:, :, :, None], (g, t, RW_HEAD, SCAN_ILO, SCAN_LANE_BATCH, RW_HEADS))
    return x.reshape(g, t, RW_HEAD, LANES)


def _to_value_lanes(x, bsz, t):
    g = bsz // SCAN_LANE_BATCH
    x = x.reshape(g, SCAN_LANE_BATCH, t, RW_HEADS, SCAN_IHI, SCAN_ILO).transpose(0, 2, 4, 5, 1, 3)
    return x.reshape(g, t, SCAN_IHI, LANES)


def _from_value_lanes(y, bsz, t):
    g = bsz // SCAN_LANE_BATCH
    y = y.reshape(g, t, SCAN_IHI, SCAN_ILO, SCAN_LANE_BATCH, RW_HEADS).transpose(0, 4, 1, 5, 2, 3)
    return y.reshape(bsz * t, RW_WIDTH)


def _state_to_lanes(s):
    bsz = s.shape[0]
    g = bsz // SCAN_LANE_BATCH
    s = s.reshape(g, SCAN_LANE_BATCH, RW_HEADS, SCAN_IHI, SCAN_ILO, RW_HEAD).transpose(0, 3, 5, 4, 1, 2)
    return s.reshape(g, SCAN_IHI, RW_HEAD, LANES)


def _state_from_lanes(s, bsz):
    g = bsz // SCAN_LANE_BATCH
    s = s.reshape(g, SCAN_IHI, RW_HEAD, SCAN_ILO, SCAN_LANE_BATCH, RW_HEADS).transpose(0, 4, 5, 1, 3, 2)
    return s.reshape(bsz, RW_HEADS, RW_HEAD, RW_HEAD)


def _merge_kernel(x_ref, ys_ref, bonus_ref, g_ref, yb_ref, ga_ref, gb_ref,
                  lnw_ref, lnb_ref, ones_ref, wpa_ref, wpb_ref, wout_ref, o_ref):
    ones = ones_ref[...]
    y = ys_ref[...]
    inv_n = 1.0 / RW_HEAD
    mu = _seg_sum(y, ones) * inv_n
    d = y - mu
    var = _seg_sum(d * d, ones) * inv_n
    ya = d * lax.rsqrt(var + RW_GN_EPS) * lnw_ref[...] + lnb_ref[...]
    ya = (ya + bonus_ref[...]) * g_ref[...]
    merged = (ga_ref[...] * _dot(ya.astype(BF16), wpa_ref[...])
              + gb_ref[...] * _dot(yb_ref[...].astype(BF16), wpb_ref[...]))
    o_ref[...] = x_ref[...] + _dot(merged.astype(BF16), wout_ref[...])


def _merge(x, ys, bonus, g, yb, gates, p, tm=256):
    n, d = x.shape
    tm = min(tm, n)
    row = lambda i: (i, 0)
    half = pl.BlockSpec((tm, RW_WIDTH), row)
    return pl.pallas_call(
        _merge_kernel,
        grid=(n // tm,),
        in_specs=[pl.BlockSpec((tm, d), row), half, half, half, half,
                  pl.BlockSpec((tm, d), lambda i: (i, 0)), pl.BlockSpec((tm, d), lambda i: (i, 1)),
                  _const_spec((1, RW_WIDTH)), _const_spec((1, RW_WIDTH)), _const_spec((RW_WIDTH, RW_WIDTH)),
                  _const_spec((RW_WIDTH, d)), _const_spec((DA_V_WIDTH, d)), _const_spec((d, d))],
        out_specs=pl.BlockSpec((tm, d), row),
        out_shape=jax.ShapeDtypeStruct((n, d), F32),
        compiler_params=_params(("parallel",)),
        name="merge",
    )(x, ys, bonus, g, yb, gates, gates, p["ln_w"], p["ln_b"], p["ones"], p["wpa"], p["wpb"], p["wout"])


def _flash_kernel(lam_ref, q_ref, k_ref, v_ref, sub_ref, o_ref, m_scr, l_scr, acc_scr, *, tq, out_scale):
    qi = pl.program_id(2)
    q = q_ref[...]
    lane = lax.broadcasted_iota(jnp.int32, q.shape, 1)
    zero = jnp.zeros_like(q)
    qs = (jnp.where(lane < DA_QK, q, zero), jnp.where(lane >= DA_QK, q, zero))
    m_scr[...] = jnp.full_like(m_scr, -jnp.inf)
    l_scr[...] = jnp.zeros_like(l_scr)
    acc_scr[...] = jnp.zeros_like(acc_scr)
    nt = (((1,), (1,)), ((), ()))

    def block(j, masked):
        off = pl.multiple_of(j * tq, tq)
        kb = k_ref[pl.ds(off, tq), :]
        vb = v_ref[pl.ds(off, tq), :]
        for mi in range(2):
            s = lax.dot_general(qs[mi], kb, nt, preferred_element_type=F32)
            if masked:
                rr = lax.broadcasted_iota(jnp.int32, s.shape, 0)
                cc = lax.broadcasted_iota(jnp.int32, s.shape, 1)
                s = jnp.where(cc <= rr, s, -jnp.inf)
            m_old = m_scr[mi]
            m_new = jnp.maximum(m_old, jnp.max(s, axis=-1, keepdims=True))
            alpha = jnp.exp(m_old - m_new)
            p = jnp.exp(s - m_new)
            l_scr[mi] = alpha * l_scr[mi] + jnp.sum(p, axis=-1, keepdims=True)
            acc_scr[mi] = alpha * acc_scr[mi] + _dot(p.astype(BF16), vb)
            m_scr[mi] = m_new

    def body(j, carry):
        block(j, False)
        return carry

    lax.fori_loop(0, qi, body, 0)
    block(qi, True)

    lam = lam_ref[...]
    o = acc_scr[0] / l_scr[0] - lam * (acc_scr[1] / l_scr[1])
    o_ref[...] = _rms(o, sub_ref[...]) * out_scale


def _flash(lam_row, qb, kb, vb, subln, bsz, s, out_scale, tq=256):
    tq = min(tq, s)
    q3 = qb.reshape(bsz, s, DA_QK_WIDTH)
    k3 = kb.reshape(bsz, s, DA_QK_WIDTH)
    v3 = vb.reshape(bsz, s, DA_V_WIDTH)
    out = pl.pallas_call(
        functools.partial(_flash_kernel, tq=tq, out_scale=out_scale),
        grid=(bsz, DA_HEADS, s // tq),
        in_specs=[_const_spec((1, LANES)),
                  pl.BlockSpec((None, tq, LANES), lambda b, h, i: (b, i, h)),
                  pl.BlockSpec((None, s, LANES), lambda b, h, i: (b, 0, h)),
                  pl.BlockSpec((None, s, DA_V), lambda b, h, i: (b, 0, h)),
                  pl.BlockSpec((1, DA_V), lambda b, h, i: (0, h))],
        out_specs=pl.BlockSpec((None, tq, DA_V), lambda b, h, i: (b, i, h)),
        out_shape=jax.ShapeDtypeStruct((bsz, s, DA_V_WIDTH), F32),
        scratch_shapes=[pltpu.VMEM((2, tq, 1), F32), pltpu.VMEM((2, tq, 1), F32),
                        pltpu.VMEM((2, tq, DA_V), F32)],
        compiler_params=_params(("parallel", "parallel", "arbitrary")),
        name="flash_diff",
    )(lam_row, q3, k3, v3, subln)
    return out.reshape(bsz * s, DA_V_WIDTH)


def _decode_kernel(pt_ref, lam_ref, q_ref, kn_ref, vn_ref, sub_ref, *rest, pages, out_scale):
    k_refs = rest[:pages]
    v_refs = rest[pages:2 * pages]
    o_ref, m_scr, l_scr, acc_scr = rest[2 * pages:]
    j = pl.program_id(1)
    q = q_ref[...]
    row = lax.broadcasted_iota(jnp.int32, (2 * DA_HEADS, LANES), 0)

    @pl.when(j == 0)
    def _():
        m_scr[...] = jnp.sum(q * kn_ref[...], axis=-1, keepdims=True)
        l_scr[...] = jnp.ones_like(l_scr)
        acc_scr[...] = vn_ref[...]

    qb = q.astype(BF16)
    nt = (((1,), (1,)), ((), ()))
    for u in range(pages):
        s = jnp.zeros((2 * DA_HEADS, LANES), F32)
        for h in range(2 * DA_HEADS):
            kh = k_refs[u][:, h, :].astype(BF16)
            sh = lax.dot_general(qb, kh, nt, preferred_element_type=F32)
            s = jnp.where(row == h, sh, s)
        m_old = m_scr[...]
        m_new = jnp.maximum(m_old, jnp.max(s, axis=-1, keepdims=True))
        alpha = jnp.exp(m_old - m_new)
        p = jnp.exp(s - m_new)
        l_scr[...] = alpha * l_scr[...] + jnp.sum(p, axis=-1, keepdims=True)
        pb = p.astype(BF16)
        pv = jnp.zeros((2 * DA_HEADS, DA_V), F32)
        for hv in range(DA_HEADS):
            vh = v_refs[u][:, hv, :].astype(BF16)
            pv = jnp.where((row >> 1) == hv, _dot(pb, vh), pv)
        acc_scr[...] = alpha * acc_scr[...] + pv
        m_scr[...] = m_new

    @pl.when(j == pl.num_programs(1) - 1)
    def _():
        o8 = acc_scr[...] / l_scr[...]
        o = o8 - lam_ref[...] * pltpu.roll(o8, 2 * DA_HEADS - 1, 0)
        o_ref[...] = _rms(o, sub_ref[...]) * out_scale


def _decode_attn(lam_row, q8, k_new8, v_new8, subln8, cache_k, cache_v, page_table, out_scale, pages=8):
    bsz, n_pages = page_table.shape
    pages = min(pages, n_pages)
    pt = page_table.reshape(-1)

    def page_map(u):
        return lambda b, j, pt_ref: (pt_ref[b * n_pages + j * pages + u], 0, 0, 0)

    seq = lambda b, j, pt_ref: (b, 0, 0)
    fixed = lambda b, j, pt_ref: (0, 0)
    in_specs = [pl.BlockSpec((1, LANES), fixed),
                pl.BlockSpec((None, 2 * DA_HEADS, DA_QK), seq),
                pl.BlockSpec((None, 2 * DA_HEADS, DA_QK), seq),
                pl.BlockSpec((None, 2 * DA_HEADS, DA_V), seq),
                pl.BlockSpec((2 * DA_HEADS, DA_V), fixed)]
    in_specs += [pl.BlockSpec((None, PAGE_SIZE, 2 * DA_HEADS, DA_QK), page_map(u)) for u in range(pages)]
    in_specs += [pl.BlockSpec((None, PAGE_SIZE, DA_HEADS, DA_V), page_map(u)) for u in range(pages)]
    grid_spec = pltpu.PrefetchScalarGridSpec(
        num_scalar_prefetch=1,
        grid=(bsz, n_pages // pages),
        in_specs=in_specs,
        out_specs=pl.BlockSpec((None, 2 * DA_HEADS, DA_V), seq),
        scratch_shapes=[pltpu.VMEM((2 * DA_HEADS, 1), F32), pltpu.VMEM((2 * DA_HEADS, 1), F32),
                        pltpu.VMEM((2 * DA_HEADS, DA_V), F32)])
    return pl.pallas_call(
        functools.partial(_decode_kernel, pages=pages, out_scale=out_scale),
        grid_spec=grid_spec,
        out_shape=jax.ShapeDtypeStruct((bsz, 2 * DA_HEADS, DA_V), F32),
        compiler_params=_params(("parallel", "arbitrary")),
        name="decode_attn",
    )(pt, lam_row, q8, k_new8, v_new8, subln8, *([cache_k] * pages), *([cache_v] * pages))


def _rope_tables(pos):
    half = DA_QK // 2
    inv = ROPE_THETA ** (-jnp.arange(half, dtype=F32) / half)
    ang = pos.astype(F32)[:, None] * inv[None, :]
    cos, sin = jnp.cos(ang), jnp.sin(ang)
    reps = LANES // DA_QK
    return jnp.tile(jnp.concatenate([cos, cos], -1), (1, reps)), jnp.tile(jnp.concatenate([-sin, sin], -1), (1, reps))


def _layer_params(l, ffn1_norm, ffn1_up, ffn1_down, mix_norm, w_in, gate_bias, rw_mu, rw_w0, rw_w2, rw_a0,
                  rw_a2, rw_g2, rw_k_k, rw_k_a, rw_r_k, rw_ln_w, rw_ln_b, da_subln, w_proj_a, w_proj_b, w_out,
                  ffn2_norm, ffn2_up, ffn2_down):
    def ffn_w(up, down):
        d, two_ff = up.shape
        ff = two_ff // 2
        tf = 256
        nc = ff // tf
        split = lambda w: w.reshape(d, nc, tf).transpose(1, 0, 2).astype(BF16)
        return split(up[:, :ff]), split(up[:, ff:]), down.reshape(nc, tf, d).astype(BF16)

    row = lambda v: v.reshape(1, -1)
    head_id = jnp.arange(RW_WIDTH) // RW_HEAD
    zeros = jnp.zeros((DECAY_LORA, RW_WIDTH), F32)
    return dict(
        ffn1=(row(ffn1_norm[l]),) + ffn_w(ffn1_up[l], ffn1_down[l]),
        ffn2=(row(ffn2_norm[l]),) + ffn_w(ffn2_up[l], ffn2_down[l]),
        mix_norm=row(mix_norm[l]), w_in=w_in[l].astype(BF16), gate_bias=row(gate_bias[l]),
        mu=row(rw_mu[l]), w0=row(rw_w0[l]), a0=row(rw_a0[l]), k_k=row(rw_k_k[l]), k_a=row(rw_k_a[l]),
        r_k=row(rw_r_k[l]),
        w2=jnp.concatenate([rw_w2[l], zeros], 0).astype(BF16),
        a2=jnp.concatenate([zeros, rw_a2[l]], 0).astype(BF16),
        g2=rw_g2[l].astype(BF16),
        ones=(head_id[:, None] == head_id[None, :]).astype(BF16),
        ln_w=row(rw_ln_w[l]), ln_b=row(rw_ln_b[l]), subln=row(da_subln[l]),
        wpa=w_proj_a[l].astype(BF16), wpb=w_proj_b[l].astype(BF16), wout=w_out[l].astype(BF16))


def _trunk_front(x, p, cos_t, sin_t, rope_blocks, z_last, bsz, t):
    x1 = _ffn(x, *p["ffn1"])
    z, qb, k, v, kb, vb, gates = _mixproj(x1, p["mix_norm"], p["w_in"], p["gate_bias"], cos_t, sin_t, rope_blocks)
    z3 = z.reshape(bsz, t, RW_COLS)
    zprev = jnp.concatenate([z_last[:, None, :], z3[:, :-1]], axis=1).reshape(bsz * t, RW_COLS)
    r, dec, kmod, vv, an, bn, g, bonus = _rwprep(z, zprev, p)
    return x1, z3, qb, k, v, kb, vb, gates, (r, dec, kmod, vv, an, bn), g, bonus


def _rwkv_scan(rw, wkv0, bsz, t):
    r, dec, kmod, vv, an, bn = rw
    key = lambda u: _to_key_lanes(u, bsz, t)
    y, s_fin = _wkv_scan(key(an), key(dec), key(bn), key(kmod), key(r), _to_value_lanes(vv, bsz, t),
                         _state_to_lanes(wkv0))
    return _from_value_lanes(y, bsz, t), _state_from_lanes(s_fin, bsz)


def kernel(x_prompt, x_sample, cache_k, cache_v, state_wkv, state_shift, page_table, ffn1_norm, ffn1_up, ffn1_down, mix_norm, w_in, gate_bias, rw_mu, rw_w0, rw_w2, rw_a0, rw_a2, rw_g2, rw_k_k, rw_k_a, rw_r_k, rw_ln_w, rw_ln_b, da_lq1, da_lk1, da_lq2, da_lk2, da_subln, w_proj_a, w_proj_b, w_out, ffn2_norm, ffn2_up, ffn2_down, final_norm):
    bp, sp, d = x_prompt.shape
    bs, ts, _ = x_sample.shape
    depth = ffn1_norm.shape[0]
    assert ts == 1 and bp % SCAN_LANE_BATCH == 0 and bs % SCAN_LANE_BATCH == 0
    n_past = page_table.shape[1] * PAGE_SIZE
    cos_p, sin_p = _rope_tables(jnp.arange(sp, dtype=jnp.int32))
    cos_s, sin_s = _rope_tables(jnp.full((bs,), n_past, dtype=jnp.int32))
    final_row = final_norm.reshape(1, d)

    hp = x_prompt.reshape(bp * sp, d)
    hs = x_sample.reshape(bs * ts, d)
    outs = [[] for _ in range(8)]
    for l in range(depth):
        p = _layer_params(l, ffn1_norm, ffn1_up, ffn1_down, mix_norm, w_in, gate_bias, rw_mu, rw_w0, rw_w2,
                          rw_a0, rw_a2, rw_g2, rw_k_k, rw_k_a, rw_r_k, rw_ln_w, rw_ln_b, da_subln, w_proj_a,
                          w_proj_b, w_out, ffn2_norm, ffn2_up, ffn2_down)
        lam_init = 0.8 - 0.6 * math.exp(-0.3 * l)
        lam = (jnp.exp(jnp.sum(da_lq1[l] * da_lk1[l])) - jnp.exp(jnp.sum(da_lq2[l] * da_lk2[l])) + lam_init)
        lam_row = jnp.full((1, LANES), lam, F32)
        out_scale = 1.0 - lam_init
        last = l == depth - 1

        tm_rope = min(256, bp * sp)
        x1, z3, qb, k, v, kb, vb, gates, rw, g, bonus = _trunk_front(
            hp, p, cos_p, sin_p, sp // min(tm_rope, sp), jnp.zeros((bp, RW_COLS), F32), bp, sp)
        ys, wkv_p = _rwkv_scan(rw, jnp.zeros((bp, RW_HEADS, RW_HEAD, RW_HEAD), F32), bp, sp)
        yb = _flash(lam_row, qb, kb, vb, p["subln"], bp, sp, out_scale)
        x2 = _merge(x1, ys, bonus, g, yb, gates, p)
        hp = _ffn(x2, *p["ffn2"], final_g=final_row if last else None)
        outs[0].append(k.reshape(bp, sp, 2 * DA_HEADS, DA_QK))
        outs[1].append(v.reshape(bp, sp, DA_HEADS, DA_V))
        outs[2].append(wkv_p)
        outs[3].append(z3[:, -1])

        x1, z3, qb, k, v, kb, vb, gates, rw, g, bonus = _trunk_front(
            hs, p, cos_s, sin_s, 1, state_shift[l], bs, ts)
        ys, wkv_s = _rwkv_scan(rw, state_wkv[l], bs, ts)
        q8 = qb.astype(F32).reshape(bs, 2 * DA_HEADS, DA_QK)
        k8 = k.reshape(bs, 2 * DA_HEADS, DA_QK)
        v8 = jnp.repeat(v.reshape(bs, DA_HEADS, DA_V), 2, axis=1)
        sub8 = jnp.repeat(p["subln"].reshape(DA_HEADS, DA_V), 2, axis=0)
        o8 = _decode_attn(lam_row, q8, k8, v8, sub8, cache_k[l], cache_v[l], page_table, out_scale)
        yb = o8[:, ::2].reshape(bs, DA_V_WIDTH)
        x2 = _merge(x1, ys, bonus, g, yb, gates, p)
        hs = _ffn(x2, *p["ffn2"], final_g=final_row if last else None)
        outs[4].append(k.reshape(bs, ts, 2 * DA_HEADS, DA_QK))
        outs[5].append(v.reshape(bs, ts, DA_HEADS, DA_V))
        outs[6].append(wkv_s)
        outs[7].append(z3[:, -1])

    st = [jnp.stack(o) for o in outs]
    return (hp.reshape(bp, sp, d), hs.reshape(bs, ts, d), st[0], st[1], st[2], st[3], st[4], st[5], st[6], st[7])
```

```python
import functools
import math

import jax
import jax.numpy as jnp
from jax import lax
from jax.experimental import pallas as pl
from jax.experimental.pallas import tpu as pltpu

F32 = jnp.float32
BF16 = jnp.bfloat16

NORM_EPS = 1e-6
ROPE_THETA = 10000.0
PAGE_SIZE = 128

RW_HEADS = 8
RW_HEAD = 64
RW_WIDTH = RW_HEADS * RW_HEAD
DECAY_LORA = 64
AAA_LORA = 64
GATE_LORA = 128
RW_COLS = 3 * RW_WIDTH + DECAY_LORA + AAA_LORA + GATE_LORA
RW_GN_EPS = 1e-5 * RW_HEAD * RW_HEAD
DA_HEADS = 4
DA_QK = 64
DA_V = 2 * DA_QK
DA_QK_WIDTH = 2 * DA_HEADS * DA_QK
DA_V_WIDTH = DA_HEADS * DA_V

LANES = 128
SCAN_LANE_BATCH = 4
SCAN_ILO = LANES // (SCAN_LANE_BATCH * RW_HEADS)
SCAN_IHI = RW_HEAD // SCAN_ILO
GROUP_UNROLL = 4
VMEM_LIMIT = 56 * 1024 * 1024


def _const_spec(shape):
    nd = len(shape)
    return pl.BlockSpec(shape, lambda *_: (0,) * nd, pipeline_mode=pl.Buffered(1))


def _params(sem):
    return pltpu.CompilerParams(dimension_semantics=sem, vmem_limit_bytes=VMEM_LIMIT)


def _rms(x, g):
    return x * lax.rsqrt(jnp.mean(x * x, axis=-1, keepdims=True) + NORM_EPS) * g


def _dot(a, b):
    return jnp.dot(a, b, preferred_element_type=F32)


def _seg_sum(x, ones_bf):
    hi = x.astype(BF16)
    lo = (x - hi.astype(F32)).astype(BF16)
    return _dot(hi, ones_bf) + _dot(lo, ones_bf)


def _ffn_kernel(x_ref, g_ref, wg_ref, wu_ref, wd_ref, *rest, final):
    if final:
        gf_ref, o_ref, h_scr, acc_scr = rest
    else:
        o_ref, h_scr, acc_scr = rest
    x = x_ref[...]
    h_scr[...] = _rms(x, g_ref[...]).astype(BF16)
    acc_scr[...] = jnp.zeros_like(acc_scr)

    def body(c, carry):
        hb = h_scr[...]
        gate = _dot(hb, wg_ref[c])
        up = _dot(hb, wu_ref[c])
        act = (gate * jax.nn.sigmoid(gate) * up).astype(BF16)
        acc_scr[...] += _dot(act, wd_ref[c])
        return carry

    lax.fori_loop(0, wg_ref.shape[0], body, 0)
    y = x_ref[...] + 0.5 * acc_scr[...]
    if final:
        y = _rms(y, gf_ref[...])
    o_ref[...] = y


def _ffn(x, norm_g, wg, wu, wd, final_g=None, tm=512):
    n, d = x.shape
    nc, _, tf = wg.shape
    tm = min(tm, n)
    final = final_g is not None
    in_specs = [pl.BlockSpec((tm, d), lambda i: (i, 0)), _const_spec((1, d)),
                _const_spec((nc, d, tf)), _const_spec((nc, d, tf)), _const_spec((nc, tf, d))]
    args = [x, norm_g, wg, wu, wd]
    if final:
        in_specs.append(_const_spec((1, d)))
        args.append(final_g)
    return pl.pallas_call(
        functools.partial(_ffn_kernel, final=final),
        grid=(n // tm,),
        in_specs=in_specs,
        out_specs=pl.BlockSpec((tm, d), lambda i: (i, 0)),
        out_shape=jax.ShapeDtypeStruct((n, d), F32),
        scratch_shapes=[pltpu.VMEM((tm, d), BF16), pltpu.VMEM((tm, d), F32)],
        compiler_params=_params(("parallel",)),
        name="ffn_final" if final else "ffn",
    )(*args)


def _mixproj_kernel(x_ref, g_ref, w_ref, gb_ref, cos_ref, sin_ref,
                    z_ref, q_ref, k_ref, v_ref, kb_ref, vb_ref, gate_ref):
    hb = _rms(x_ref[...], g_ref[...]).astype(BF16)
    for c in range(0, RW_COLS, 256):
        z_ref[:, c:c + 256] = _dot(hb, w_ref[:, c:c + 256])

    cos = cos_ref[...]
    sin = sin_ref[...]
    lane = lax.broadcasted_iota(jnp.int32, cos.shape, 1)
    first_half = (lane & (DA_QK // 2)) == 0

    def rope(xx):
        swapped = jnp.where(first_half, pltpu.roll(xx, LANES - DA_QK // 2, 1), pltpu.roll(xx, DA_QK // 2, 1))
        return xx * cos + swapped * sin

    q_off = RW_COLS
    k_off = q_off + DA_QK_WIDTH
    v_off = k_off + DA_QK_WIDTH
    g_off = v_off + DA_V_WIDTH
    q = _dot(hb, w_ref[:, q_off:q_off + DA_QK_WIDTH])
    k = _dot(hb, w_ref[:, k_off:k_off + DA_QK_WIDTH])
    for c in range(0, DA_QK_WIDTH, LANES):
        q_ref[:, c:c + LANES] = (rope(q[:, c:c + LANES]) * (DA_QK ** -0.5)).astype(BF16)
        kr = rope(k[:, c:c + LANES])
        k_ref[:, c:c + LANES] = kr
        kb_ref[:, c:c + LANES] = kr.astype(BF16)
    v = _dot(hb, w_ref[:, v_off:v_off + DA_V_WIDTH])
    v_ref[...] = v
    vb_ref[...] = v.astype(BF16)
    for c in range(0, gate_ref.shape[1], 512):
        gate_ref[:, c:c + 512] = jax.nn.sigmoid(
            _dot(hb, w_ref[:, g_off + c:g_off + c + 512]) + gb_ref[:, c:c + 512])


def _mixproj(x, norm_g, w_in, gate_bias, cos_t, sin_t, rope_blocks, tm=256):
    n, d = x.shape
    in_cols = w_in.shape[1]
    gate_cols = gate_bias.shape[1]
    tm = min(tm, n)
    row = lambda i: (i, 0)
    rope_map = lambda i: (i % rope_blocks, 0)
    outs = [(RW_COLS, F32), (DA_QK_WIDTH, BF16), (DA_QK_WIDTH, F32), (DA_V_WIDTH, F32),
            (DA_QK_WIDTH, BF16), (DA_V_WIDTH, BF16), (gate_cols, F32)]
    return pl.pallas_call(
        _mixproj_kernel,
        grid=(n // tm,),
        in_specs=[pl.BlockSpec((tm, d), row), _const_spec((1, d)), _const_spec((d, in_cols)),
                  _const_spec((1, gate_cols)),
                  pl.BlockSpec((tm, LANES), rope_map), pl.BlockSpec((tm, LANES), rope_map)],
        out_specs=[pl.BlockSpec((tm, w), row) for w, _ in outs],
        out_shape=[jax.ShapeDtypeStruct((n, w), dt) for w, dt in outs],
        compiler_params=_params(("parallel",)),
        name="mixproj",
    )(x, norm_g, w_in, gate_bias, cos_t, sin_t)


def _rwprep_kernel(z_ref, zp_ref, mu_ref, w0_ref, a0_ref, kk_ref, ka_ref, rk_ref,
                   w2_ref, a2_ref, g2_ref, ones_ref,
                   r_ref, dec_ref, k_ref, v_ref, an_ref, bn_ref, g_ref, bonus_ref):
    z = z_ref[...]
    zm = z + (zp_ref[...] - z) * mu_ref[...]
    o1 = RW_WIDTH
    r = zm[:, :o1]
    k = zm[:, o1:2 * o1]
    v = zm[:, 2 * o1:3 * o1]
    lora_in = zm[:, 3 * o1:3 * o1 + LANES]
    gd = zm[:, 3 * o1 + LANES:]
    ones = ones_ref[...]

    x = w0_ref[...] + _dot(jnp.tanh(lora_in).astype(BF16), w2_ref[...])
    neg = -x
    softplus = jnp.maximum(neg, 0.0) + jnp.log1p(jnp.exp(-jnp.abs(neg)))
    dec_ref[...] = jnp.exp(-jnp.exp(-softplus - 0.5))
    a = jax.nn.sigmoid(a0_ref[...] + _dot(lora_in.astype(BF16), a2_ref[...]))
    g_ref[...] = _dot(jax.nn.sigmoid(gd).astype(BF16), g2_ref[...])

    kk = k * kk_ref[...]
    norm = jnp.sqrt(_seg_sum(kk * kk, ones))
    kk = kk / jnp.maximum(norm, 1e-12)
    kmod = k * (1.0 + (a - 1.0) * ka_ref[...])
    r_ref[...] = r
    k_ref[...] = kmod
    v_ref[...] = v
    an_ref[...] = -kk
    bn_ref[...] = kk * a
    bonus_ref[...] = _seg_sum(r * kmod * rk_ref[...], ones) * v


def _rwprep(z, zprev, p, tm=256):
    n = z.shape[0]
    tm = min(tm, n)
    row = lambda i: (i, 0)
    vec = _const_spec((1, RW_WIDTH))
    return pl.pallas_call(
        _rwprep_kernel,
        grid=(n // tm,),
        in_specs=[pl.BlockSpec((tm, RW_COLS), row), pl.BlockSpec((tm, RW_COLS), row),
                  _const_spec((1, RW_COLS)), vec, vec, vec, vec, vec,
                  _const_spec((LANES, RW_WIDTH)), _const_spec((LANES, RW_WIDTH)),
                  _const_spec((GATE_LORA, RW_WIDTH)), _const_spec((RW_WIDTH, RW_WIDTH))],
        out_specs=[pl.BlockSpec((tm, RW_WIDTH), row)] * 8,
        out_shape=[jax.ShapeDtypeStruct((n, RW_WIDTH), F32)] * 8,
        compiler_params=_params(("parallel",)),
        name="rwprep",
    )(z, zprev, p["mu"], p["w0"], p["a0"], p["k_k"], p["k_a"], p["r_k"],
      p["w2"], p["a2"], p["g2"], p["ones"])


def _scan_kernel(a_ref, w_ref, b_ref, k_ref, r_ref, v_ref, s0_ref, spread_ref, y_ref, sf_ref,
                 s_scr, key_scr, dots_scr, row_scr, *, tpack):
    tblk = pl.program_id(1)

    @pl.when(tblk == 0)
    def _():
        s_scr[...] = s0_ref[0]

    def prepare(tq, carry):
        a, w, b, k, r = (ref[0, tq] for ref in (a_ref, w_ref, b_ref, k_ref, r_ref))
        dots = jnp.concatenate([jnp.sum(b * r, axis=0, keepdims=True), jnp.sum(k * r, axis=0, keepdims=True),
                                jnp.zeros((6, LANES), F32)], axis=0)
        x = jnp.concatenate([a, w, b, k, w * r, dots], axis=0)
        if tpack > 1:
            hi = x.astype(BF16)
            lo = (x - hi.astype(F32)).astype(BF16)
            x = _dot(jnp.concatenate([hi, lo], axis=1), spread_ref[...])
        for tl in range(tpack):
            t = tq * tpack + tl
            xt = x[:, tl * LANES:(tl + 1) * LANES]
            for qi in range(5):
                key_scr[t, qi] = xt[qi * RW_HEAD:(qi + 1) * RW_HEAD]
            dots_scr[t] = xt[5 * RW_HEAD:5 * RW_HEAD + 2]
        return carry

    lax.fori_loop(0, a_ref.shape[1], prepare, 0, unroll=2 if a_ref.shape[1] % 2 == 0 else 1)

    def step(t, carry):
        def group(ih, c):
            s = s_scr[ih]
            sa = jnp.sum(s * key_scr[t, 0], axis=0, keepdims=True)
            row_scr[1, pl.ds(ih, 1), :] = jnp.sum(s * key_scr[t, 4], axis=0, keepdims=True)
            row_scr[0, pl.ds(ih, 1), :] = sa
            s_scr[ih] = s * key_scr[t, 1] + sa * key_scr[t, 2] + v_ref[0, t, pl.ds(ih, 1), :] * key_scr[t, 3]
            return c

        lax.fori_loop(0, SCAN_IHI, group, 0, unroll=GROUP_UNROLL)
        dots = dots_scr[t]
        y_ref[0, t] = row_scr[1] + row_scr[0] * dots[0:1] + v_ref[0, t] * dots[1:2]
        return carry

    lax.fori_loop(0, v_ref.shape[1], step, 0)

    @pl.when(tblk == pl.num_programs(1) - 1)
    def _():
        sf_ref[0] = s_scr[...]


def _wkv_scan(a, w, b, k, r, v, s0, tpack, tb=64):
    n_p, t = v.shape[:2]
    tb = min(tb, t)
    keyspec = pl.BlockSpec((1, tb // tpack, RW_HEAD, LANES), lambda p, i: (p, i, 0, 0))
    valspec = pl.BlockSpec((1, tb, SCAN_IHI, LANES), lambda p, i: (p, i, 0, 0))
    stspec = pl.BlockSpec((1, SCAN_IHI, RW_HEAD, LANES), lambda p, i: (p, 0, 0, 0))
    group_w = LANES // SCAN_ILO
    out_lane = jnp.arange(SCAN_ILO * LANES)
    src_lane = (out_lane // LANES) * group_w + out_lane % group_w
    spread = jnp.tile((jnp.arange(LANES)[:, None] == src_lane[None, :]).astype(BF16), (2, 1))
    return pl.pallas_call(
        functools.partial(_scan_kernel, tpack=tpack),
        grid=(n_p, t // tb),
        in_specs=[keyspec] * 5 + [valspec, stspec, _const_spec(spread.shape)],
        out_specs=[valspec, stspec],
        out_shape=[jax.ShapeDtypeStruct((n_p, t, SCAN_IHI, LANES), F32),
                   jax.ShapeDtypeStruct((n_p, SCAN_IHI, RW_HEAD, LANES), F32)],
        scratch_shapes=[pltpu.VMEM((SCAN_IHI, RW_HEAD, LANES), F32), pltpu.VMEM((tb, 5, RW_HEAD, LANES), F32),
                        pltpu.VMEM((tb, 2, LANES), F32), pltpu.VMEM((2, SCAN_IHI, LANES), F32)],
        compiler_params=_params(("parallel", "arbitrary")),
        name="wkv_scan",
    )(a, w, b, k, r, v, s0, spread)


def _to_key_lanes(x, bsz, t, tpack):
    g = bsz // SCAN_LANE_BATCH
    if tpack == 1:
        x = x.reshape(g, SCAN_LANE_BATCH, t, RW_HEADS, RW_HEAD).transpose(0, 2, 4, 1, 3)
        x = jnp.broadcast_to(x[:, :, :, None], (g, t, RW_HEAD, SCAN_ILO, SCAN_LANE_BATCH, RW_HEADS))
        return x.reshape(g, t, RW_HEAD, LANES)
    x = x.reshape(g, SCAN_LANE_BATCH, t // tpack, tpack, RW_HEADS, RW_HEAD).transpose(0, 2, 5, 3, 1, 4)
    return x.reshape(g, t // tpack, RW_HEAD, LANES)


def _to_value_lanes(x, bsz, t):
    g = bsz // SCAN_LANE_BATCH
    x = x.reshape(g, SCAN_LANE_BATCH, t, RW_HEADS, SCAN_IHI, SCAN_ILO).transpose(0, 2, 4, 5, 1, 3)
    return x.reshape(g, t, SCAN_IHI, LANES)


def _from_value_lanes(y, bsz, t):
    g = bsz // SCAN_LANE_BATCH
    y = y.reshape(g, t, SCAN_IHI, SCAN_ILO, SCAN_LANE_BATCH, RW_HEADS).transpose(0, 4, 1, 5, 2, 3)
    return y.reshape(bsz * t, RW_WIDTH)


def _state_to_lanes(s):
    bsz = s.shape[0]
    g = bsz // SCAN_LANE_BATCH
    s = s.reshape(g, SCAN_LANE_BATCH, RW_HEADS, SCAN_IHI, SCAN_ILO, RW_HEAD).transpose(0, 3, 5, 4, 1, 2)
    return s.reshape(g, SCAN_IHI, RW_HEAD, LANES)


def _state_from_lanes(s, bsz):
    g = bsz // SCAN_LANE_BATCH
    s = s.reshape(g, SCAN_IHI, RW_HEAD, SCAN_ILO, SCAN_LANE_BATCH, RW_HEADS).transpose(0, 4, 5, 1, 3, 2)
    return s.reshape(bsz, RW_HEADS, RW_HEAD, RW_HEAD)


def _merge_kernel(x_ref, ys_ref, bonus_ref, g_ref, yb_ref, ga_ref, gb_ref,
                  lnw_ref, lnb_ref, ones_ref, wpa_ref, wpb_ref, wout_ref, o_ref):
    ones = ones_ref[...]
    y = ys_ref[...]
    inv_n = 1.0 / RW_HEAD
    mu = _seg_sum(y, ones) * inv_n
    d = y - mu
    var = _seg_sum(d * d, ones) * inv_n
    ya = d * lax.rsqrt(var + RW_GN_EPS) * lnw_ref[...] + lnb_ref[...]
    ya = (ya + bonus_ref[...]) * g_ref[...]
    merged = (ga_ref[...] * _dot(ya.astype(BF16), wpa_ref[...])
              + gb_ref[...] * _dot(yb_ref[...].astype(BF16), wpb_ref[...]))
    o_ref[...] = x_ref[...] + _dot(merged.astype(BF16), wout_ref[...])


def _merge(x, ys, bonus, g, yb, gates, p, tm=256):
    n, d = x.shape
    tm = min(tm, n)
    row = lambda i: (i, 0)
    half = pl.BlockSpec((tm, RW_WIDTH), row)
    return pl.pallas_call(
        _merge_kernel,
        grid=(n // tm,),
        in_specs=[pl.BlockSpec((tm, d), row), half, half, half, half,
                  pl.BlockSpec((tm, d), lambda i: (i, 0)), pl.BlockSpec((tm, d), lambda i: (i, 1)),
                  _const_spec((1, RW_WIDTH)), _const_spec((1, RW_WIDTH)), _const_spec((RW_WIDTH, RW_WIDTH)),
                  _const_spec((RW_WIDTH, d)), _const_spec((DA_V_WIDTH, d)), _const_spec((d, d))],
        out_specs=pl.BlockSpec((tm, d), row),
        out_shape=jax.ShapeDtypeStruct((n, d), F32),
        compiler_params=_params(("parallel",)),
        name="merge",
    )(x, ys, bonus, g, yb, gates, gates, p["ln_w"], p["ln_b"], p["ones"], p["wpa"], p["wpb"], p["wout"])


def _flash_kernel(lam_ref, q_ref, k_ref, v_ref, sub_ref, o_ref, m_scr, l_scr, acc_scr, *, tq, out_scale):
    qi = pl.program_id(2)
    q = q_ref[...]
    lane = lax.broadcasted_iota(jnp.int32, q.shape, 1)
    zero = jnp.zeros_like(q)
    q2 = jnp.concatenate([jnp.where(lane < DA_QK, q, zero), jnp.where(lane >= DA_QK, q, zero)], axis=0)
    m_scr[...] = jnp.full_like(m_scr, -jnp.inf)
    l_scr[...] = jnp.zeros_like(l_scr)
    acc_scr[...] = jnp.zeros_like(acc_scr)
    nt = (((1,), (1,)), ((), ()))
    reps = tq // LANES

    def block(j, masked):
        off = pl.multiple_of(j * tq, tq)
        kb = k_ref[pl.ds(off, tq), :]
        vb = v_ref[pl.ds(off, tq), :]
        s = lax.dot_general(q2, kb, nt, preferred_element_type=F32)
        if masked:
            rr = lax.broadcasted_iota(jnp.int32, s.shape, 0) & (tq - 1)
            cc = lax.broadcasted_iota(jnp.int32, s.shape, 1)
            s = jnp.where(cc <= rr, s, -jnp.inf)
        m_old = m_scr[...]
        m_new = jnp.maximum(m_old, jnp.max(s, axis=-1, keepdims=True))
        alpha = jnp.exp(m_old - m_new)
        p = jnp.exp(s - jnp.concatenate([m_new] * reps, axis=1))
        l_scr[...] = alpha * l_scr[...] + jnp.sum(p, axis=-1, keepdims=True)
        acc_scr[...] = alpha * acc_scr[...] + _dot(p.astype(BF16), vb)
        m_scr[...] = m_new

    def body(j, carry):
        block(j, False)
        return carry

    lax.fori_loop(0, qi, body, 0)
    block(qi, True)

    o = acc_scr[...] / l_scr[...]
    o = o[:tq] - lam_ref[...] * o[tq:]
    o_ref[...] = _rms(o, sub_ref[...]) * out_scale


def _flash(lam_row, qb, kb, vb, subln, bsz, s, out_scale, tq=512):
    tq = min(tq, s)
    q3 = qb.reshape(bsz, s, DA_QK_WIDTH)
    k3 = kb.reshape(bsz, s, DA_QK_WIDTH)
    v3 = vb.reshape(bsz, s, DA_V_WIDTH)
    out = pl.pallas_call(
        functools.partial(_flash_kernel, tq=tq, out_scale=out_scale),
        grid=(bsz, DA_HEADS, s // tq),
        in_specs=[_const_spec((1, LANES)),
                  pl.BlockSpec((None, tq, LANES), lambda b, h, i: (b, i, h)),
                  pl.BlockSpec((None, s, LANES), lambda b, h, i: (b, 0, h)),
                  pl.BlockSpec((None, s, DA_V), lambda b, h, i: (b, 0, h)),
                  pl.BlockSpec((1, DA_V), lambda b, h, i: (0, h))],
        out_specs=pl.BlockSpec((None, tq, DA_V), lambda b, h, i: (b, i, h)),
        out_shape=jax.ShapeDtypeStruct((bsz, s, DA_V_WIDTH), F32),
        scratch_shapes=[pltpu.VMEM((2 * tq, LANES), F32), pltpu.VMEM((2 * tq, LANES), F32),
                        pltpu.VMEM((2 * tq, DA_V), F32)],
        compiler_params=_params(("parallel", "parallel", "arbitrary")),
        name="flash_diff",
    )(lam_row, q3, k3, v3, subln)
    return out.reshape(bsz * s, DA_V_WIDTH)


def _decode_kernel(pt_ref, lam_ref, q_ref, qc_ref, kn_ref, vn_ref, sub_ref, expand_ref, *rest, pages, out_scale):
    k_refs = rest[:pages]
    v_refs = rest[pages:2 * pages]
    o_ref, m_scr, l_scr, acc_scr, qc_scr = rest[2 * pages:]
    j = pl.program_id(1)
    vrows = (2 * DA_HEADS, PAGE_SIZE * DA_HEADS)
    keep = (lax.broadcasted_iota(jnp.int32, vrows, 1) % DA_HEADS) == (lax.broadcasted_iota(jnp.int32, vrows, 0) >> 1)

    @pl.when(j == 0)
    def _():
        m_scr[...] = jnp.sum(q_ref[...] * kn_ref[...], axis=-1, keepdims=True)
        l_scr[...] = jnp.ones_like(l_scr)
        acc_scr[...] = vn_ref[...]
        for h in range(2 * DA_HEADS):
            qc_scr[h] = jnp.broadcast_to(qc_ref[h], (DA_QK, LANES))

    s = jnp.concatenate(
        [jnp.concatenate([jnp.sum(k_refs[u][h] * qc_scr[h], axis=0, keepdims=True)
                          for h in range(2 * DA_HEADS)], axis=0) for u in range(pages)], axis=1)
    m_old = m_scr[...]
    m_new = jnp.maximum(m_old, jnp.max(s, axis=-1, keepdims=True))
    alpha = jnp.exp(m_old - m_new)
    p = jnp.exp(s - m_new)
    l_scr[...] = alpha * l_scr[...] + jnp.sum(p, axis=-1, keepdims=True)
    m_scr[...] = m_new
    pv = jnp.zeros(acc_scr.shape, F32)
    for u in range(pages):
        pu = p[:, u * PAGE_SIZE:(u + 1) * PAGE_SIZE].astype(BF16)
        pe = jnp.where(keep, _dot(pu, expand_ref[...]), 0.0).astype(BF16)
        pv = pv + _dot(pe, v_refs[u][...].astype(BF16))
    acc_scr[...] = alpha * acc_scr[...] + pv

    @pl.when(j == pl.num_programs(1) - 1)
    def _():
        o8 = acc_scr[...] / l_scr[...]
        o = o8 - lam_ref[...] * pltpu.roll(o8, 2 * DA_HEADS - 1, 0)
        o_ref[...] = _rms(o, sub_ref[...]) * out_scale


def _decode_attn(lam_row, q8, k_new8, v_new8, subln8, cache_kt, cache_v, page_table, out_scale, pages=8):
    bsz, n_pages = page_table.shape
    pages = min(pages, n_pages)
    pt = page_table.reshape(-1)

    def page_map(u):
        return lambda b, j, pt_ref: (pt_ref[b * n_pages + j * pages + u], 0, 0, 0)

    def v_page_map(u):
        return lambda b, j, pt_ref: (pt_ref[b * n_pages + j * pages + u], 0, 0)

    v_rows = jnp.arange(PAGE_SIZE * DA_HEADS) // DA_HEADS
    expand = (jnp.arange(PAGE_SIZE)[:, None] == v_rows[None, :]).astype(BF16)
    seq = lambda b, j, pt_ref: (b, 0, 0)
    fixed = lambda b, j, pt_ref: (0, 0)
    in_specs = [pl.BlockSpec((1, LANES), fixed),
                pl.BlockSpec((None, 2 * DA_HEADS, DA_QK), seq),
                pl.BlockSpec((None, 2 * DA_HEADS, DA_QK, 1), lambda b, j, pt_ref: (b, 0, 0, 0)),
                pl.BlockSpec((None, 2 * DA_HEADS, DA_QK), seq),
                pl.BlockSpec((None, 2 * DA_HEADS, DA_V), seq),
                pl.BlockSpec((2 * DA_HEADS, DA_V), fixed),
                pl.BlockSpec((PAGE_SIZE, PAGE_SIZE * DA_HEADS), fixed)]
    in_specs += [pl.BlockSpec((None, 2 * DA_HEADS, DA_QK, PAGE_SIZE), page_map(u)) for u in range(pages)]
    in_specs += [pl.BlockSpec((None, PAGE_SIZE * DA_HEADS, DA_V), v_page_map(u)) for u in range(pages)]
    grid_spec = pltpu.PrefetchScalarGridSpec(
        num_scalar_prefetch=1,
        grid=(bsz, n_pages // pages),
        in_specs=in_specs,
        out_specs=pl.BlockSpec((None, 2 * DA_HEADS, DA_V), seq),
        scratch_shapes=[pltpu.VMEM((2 * DA_HEADS, 1), F32), pltpu.VMEM((2 * DA_HEADS, 1), F32),
                        pltpu.VMEM((2 * DA_HEADS, DA_V), F32),
                        pltpu.VMEM((2 * DA_HEADS, DA_QK, LANES), F32)])
    return pl.pallas_call(
        functools.partial(_decode_kernel, pages=pages, out_scale=out_scale),
        grid_spec=grid_spec,
        out_shape=jax.ShapeDtypeStruct((bsz, 2 * DA_HEADS, DA_V), F32),
        compiler_params=_params(("parallel", "arbitrary")),
        name="decode_attn",
    )(pt, lam_row, q8, q8[..., None], k_new8, v_new8, subln8, expand,
      *([cache_kt] * pages), *([cache_v] * pages))


def _rope_tables(pos):
    half = DA_QK // 2
    inv = ROPE_THETA ** (-jnp.arange(half, dtype=F32) / half)
    ang = pos.astype(F32)[:, None] * inv[None, :]
    cos, sin = jnp.cos(ang), jnp.sin(ang)
    reps = LANES // DA_QK
    return jnp.tile(jnp.concatenate([cos, cos], -1), (1, reps)), jnp.tile(jnp.concatenate([-sin, sin], -1), (1, reps))


def _layer_params(l, ffn1_norm, ffn1_up, ffn1_down, mix_norm, w_in, gate_bias, rw_mu, rw_w0, rw_w2, rw_a0,
                  rw_a2, rw_g2, rw_k_k, rw_k_a, rw_r_k, rw_ln_w, rw_ln_b, da_subln, w_proj_a, w_proj_b, w_out,
                  ffn2_norm, ffn2_up, ffn2_down):
    def ffn_w(up, down):
        d, two_ff = up.shape
        ff = two_ff // 2
        tf = 256
        nc = ff // tf
        split = lambda w: w.reshape(d, nc, tf).transpose(1, 0, 2).astype(BF16)
        return split(up[:, :ff]), split(up[:, ff:]), down.reshape(nc, tf, d).astype(BF16)

    row = lambda v: v.reshape(1, -1)
    head_id = jnp.arange(RW_WIDTH) // RW_HEAD
    zeros = jnp.zeros((DECAY_LORA, RW_WIDTH), F32)
    return dict(
        ffn1=(row(ffn1_norm[l]),) + ffn_w(ffn1_up[l], ffn1_down[l]),
        ffn2=(row(ffn2_norm[l]),) + ffn_w(ffn2_up[l], ffn2_down[l]),
        mix_norm=row(mix_norm[l]), w_in=w_in[l].astype(BF16), gate_bias=row(gate_bias[l]),
        mu=row(rw_mu[l]), w0=row(rw_w0[l]), a0=row(rw_a0[l]), k_k=row(rw_k_k[l]), k_a=row(rw_k_a[l]),
        r_k=row(rw_r_k[l]),
        w2=jnp.concatenate([rw_w2[l], zeros], 0).astype(BF16),
        a2=jnp.concatenate([zeros, rw_a2[l]], 0).astype(BF16),
        g2=rw_g2[l].astype(BF16),
        ones=(head_id[:, None] == head_id[None, :]).astype(BF16),
        ln_w=row(rw_ln_w[l]), ln_b=row(rw_ln_b[l]), subln=row(da_subln[l]),
        wpa=w_proj_a[l].astype(BF16), wpb=w_proj_b[l].astype(BF16), wout=w_out[l].astype(BF16))


def _trunk_front(x, p, cos_t, sin_t, rope_blocks, z_last, bsz, t):
    x1 = _ffn(x, *p["ffn1"])
    z, qb, k, v, kb, vb, gates = _mixproj(x1, p["mix_norm"], p["w_in"], p["gate_bias"], cos_t, sin_t, rope_blocks)
    z3 = z.reshape(bsz, t, RW_COLS)
    zprev = jnp.concatenate([z_last[:, None, :], z3[:, :-1]], axis=1).reshape(bsz * t, RW_COLS)
    r, dec, kmod, vv, an, bn, g, bonus = _rwprep(z, zprev, p)
    return x1, z3, qb, k, v, kb, vb, gates, (r, dec, kmod, vv, an, bn), g, bonus


def _rwkv_scan(rw, wkv0, bsz, t):
    r, dec, kmod, vv, an, bn = rw
    tpack = SCAN_ILO if t % SCAN_ILO == 0 else 1
    key = lambda u: _to_key_lanes(u, bsz, t, tpack)
    y, s_fin = _wkv_scan(key(an), key(dec), key(bn), key(kmod), key(r), _to_value_lanes(vv, bsz, t),
                         _state_to_lanes(wkv0), tpack)
    return _from_value_lanes(y, bsz, t), _state_from_lanes(s_fin, bsz)


def kernel(x_prompt, x_sample, cache_k, cache_v, state_wkv, state_shift, page_table, ffn1_norm, ffn1_up, ffn1_down, mix_norm, w_in, gate_bias, rw_mu, rw_w0, rw_w2, rw_a0, rw_a2, rw_g2, rw_k_k, rw_k_a, rw_r_k, rw_ln_w, rw_ln_b, da_lq1, da_lk1, da_lq2, da_lk2, da_subln, w_proj_a, w_proj_b, w_out, ffn2_norm, ffn2_up, ffn2_down, final_norm):
    bp, sp, d = x_prompt.shape
    bs, ts, _ = x_sample.shape
    depth = ffn1_norm.shape[0]
    assert ts == 1 and bp % SCAN_LANE_BATCH == 0 and bs % SCAN_LANE_BATCH == 0
    assert cache_k.shape[2] == PAGE_SIZE == LANES
    n_past = page_table.shape[1] * PAGE_SIZE
    cos_p, sin_p = _rope_tables(jnp.arange(sp, dtype=jnp.int32))
    cos_s, sin_s = _rope_tables(jnp.full((bs,), n_past, dtype=jnp.int32))
    final_row = final_norm.reshape(1, d)

    hp = x_prompt.reshape(bp * sp, d)
    hs = x_sample.reshape(bs * ts, d)
    outs = [[] for _ in range(8)]
    for l in range(depth):
        p = _layer_params(l, ffn1_norm, ffn1_up, ffn1_down, mix_norm, w_in, gate_bias, rw_mu, rw_w0, rw_w2,
                          rw_a0, rw_a2, rw_g2, rw_k_k, rw_k_a, rw_r_k, rw_ln_w, rw_ln_b, da_subln, w_proj_a,
                          w_proj_b, w_out, ffn2_norm, ffn2_up, ffn2_down)
        lam_init = 0.8 - 0.6 * math.exp(-0.3 * l)
        lam = (jnp.exp(jnp.sum(da_lq1[l] * da_lk1[l])) - jnp.exp(jnp.sum(da_lq2[l] * da_lk2[l])) + lam_init)
        lam_row = jnp.full((1, LANES), lam, F32)
        out_scale = 1.0 - lam_init
        last = l == depth - 1

        tm_rope = min(256, bp * sp)
        x1, z3, qb, k, v, kb, vb, gates, rw, g, bonus = _trunk_front(
            hp, p, cos_p, sin_p, sp // min(tm_rope, sp), jnp.zeros((bp, RW_COLS), F32), bp, sp)
        ys, wkv_p = _rwkv_scan(rw, jnp.zeros((bp, RW_HEADS, RW_HEAD, RW_HEAD), F32), bp, sp)
        yb = _flash(lam_row, qb, kb, vb, p["subln"], bp, sp, out_scale)
        x2 = _merge(x1, ys, bonus, g, yb, gates, p)
        hp = _ffn(x2, *p["ffn2"], final_g=final_row if last else None)
        outs[0].append(k.reshape(bp, sp, 2 * DA_HEADS, DA_QK))
        outs[1].append(v.reshape(bp, sp, DA_HEADS, DA_V))
        outs[2].append(wkv_p)
        outs[3].append(z3[:, -1])

        x1, z3, qb, k, v, kb, vb, gates, rw, g, bonus = _trunk_front(
            hs, p, cos_s, sin_s, 1, state_shift[l], bs, ts)
        ys, wkv_s = _rwkv_scan(rw, state_wkv[l], bs, ts)
        q8 = qb.astype(F32).reshape(bs, 2 * DA_HEADS, DA_QK)
        k8 = k.reshape(bs, 2 * DA_HEADS, DA_QK)
        v8 = jnp.repeat(v.reshape(bs, DA_HEADS, DA_V), 2, axis=1)
        sub8 = jnp.repeat(p["subln"].reshape(DA_HEADS, DA_V), 2, axis=0)
        cache_vf = cache_v[l].reshape(cache_v.shape[1], PAGE_SIZE * DA_HEADS, DA_V)
        o8 = _decode_attn(lam_row, q8, k8, v8, sub8, jnp.transpose(cache_k[l], (0, 2, 3, 1)), cache_vf,
                          page_table, out_scale)
        yb = o8[:, ::2].reshape(bs, DA_V_WIDTH)
        x2 = _merge(x1, ys, bonus, g, yb, gates, p)
        hs = _ffn(x2, *p["ffn2"], final_g=final_row if last else None)
        outs[4].append(k.reshape(bs, ts, 2 * DA_HEADS, DA_QK))
        outs[5].append(v.reshape(bs, ts, DA_HEADS, DA_V))
        outs[6].append(wkv_s)
        outs[7].append(z3[:, -1])

    st = [jnp.stack(o) for o in outs]
    return (hp.reshape(bp, sp, d), hs.reshape(bs, ts, d), st[0], st[1], st[2], st[3], st[4], st[5], st[6], st[7])
```

```python
import functools
import math

import jax
import jax.numpy as jnp
from jax import lax
from jax.experimental import pallas as pl
from jax.experimental.pallas import tpu as pltpu

F32 = jnp.float32
BF16 = jnp.bfloat16

NORM_EPS = 1e-6
ROPE_THETA = 10000.0
PAGE_SIZE = 128

RW_HEADS = 8
RW_HEAD = 64
RW_WIDTH = RW_HEADS * RW_HEAD
DECAY_LORA = 64
AAA_LORA = 64
GATE_LORA = 128
RW_COLS = 3 * RW_WIDTH + DECAY_LORA + AAA_LORA + GATE_LORA
RW_GN_EPS = 1e-5 * RW_HEAD * RW_HEAD
DA_HEADS = 4
DA_QK = 64
DA_V = 2 * DA_QK
DA_QK_WIDTH = 2 * DA_HEADS * DA_QK
DA_V_WIDTH = DA_HEADS * DA_V

LANES = 128
SCAN_LANE_BATCH = 4
SCAN_ILO = LANES // (SCAN_LANE_BATCH * RW_HEADS)
SCAN_IHI = RW_HEAD // SCAN_ILO
GROUP_UNROLL = 16
VMEM_LIMIT = 56 * 1024 * 1024


def _const_spec(shape):
    nd = len(shape)
    return pl.BlockSpec(shape, lambda *_: (0,) * nd, pipeline_mode=pl.Buffered(1))


def _params(sem):
    return pltpu.CompilerParams(dimension_semantics=sem, vmem_limit_bytes=VMEM_LIMIT)


def _rms(x, g):
    return x * lax.rsqrt(jnp.mean(x * x, axis=-1, keepdims=True) + NORM_EPS) * g


def _dot(a, b):
    return jnp.dot(a, b, preferred_element_type=F32)


def _seg_sum(x, ones_bf):
    hi = x.astype(BF16)
    lo = (x - hi.astype(F32)).astype(BF16)
    return _dot(hi, ones_bf) + _dot(lo, ones_bf)


def _ffn_kernel(x_ref, g_ref, wg_ref, wu_ref, wd_ref, *rest, final):
    if final:
        gf_ref, o_ref, h_scr, acc_scr = rest
    else:
        o_ref, h_scr, acc_scr = rest
    x = x_ref[...]
    h_scr[...] = _rms(x, g_ref[...]).astype(BF16)
    acc_scr[...] = jnp.zeros_like(acc_scr)

    def body(c, carry):
        hb = h_scr[...]
        gate = _dot(hb, wg_ref[c])
        up = _dot(hb, wu_ref[c])
        act = (gate * jax.nn.sigmoid(gate) * up).astype(BF16)
        acc_scr[...] += _dot(act, wd_ref[c])
        return carry

    lax.fori_loop(0, wg_ref.shape[0], body, 0)
    y = x_ref[...] + 0.5 * acc_scr[...]
    if final:
        y = _rms(y, gf_ref[...])
    o_ref[...] = y


def _ffn(x, norm_g, wg, wu, wd, final_g=None, tm=512):
    n, d = x.shape
    nc, _, tf = wg.shape
    tm = min(tm, n)
    final = final_g is not None
    in_specs = [pl.BlockSpec((tm, d), lambda i: (i, 0)), _const_spec((1, d)),
                _const_spec((nc, d, tf)), _const_spec((nc, d, tf)), _const_spec((nc, tf, d))]
    args = [x, norm_g, wg, wu, wd]
    if final:
        in_specs.append(_const_spec((1, d)))
        args.append(final_g)
    return pl.pallas_call(
        functools.partial(_ffn_kernel, final=final),
        grid=(n // tm,),
        in_specs=in_specs,
        out_specs=pl.BlockSpec((tm, d), lambda i: (i, 0)),
        out_shape=jax.ShapeDtypeStruct((n, d), F32),
        scratch_shapes=[pltpu.VMEM((tm, d), BF16), pltpu.VMEM((tm, d), F32)],
        compiler_params=_params(("parallel",)),
        name="ffn_final" if final else "ffn",
    )(*args)


def _mixproj_kernel(x_ref, g_ref, w_ref, gb_ref, cos_ref, sin_ref,
                    z_ref, q_ref, k_ref, v_ref, kb_ref, vb_ref, gate_ref):
    hb = _rms(x_ref[...], g_ref[...]).astype(BF16)
    for c in range(0, RW_COLS, 256):
        z_ref[:, c:c + 256] = _dot(hb, w_ref[:, c:c + 256])

    cos = cos_ref[...]
    sin = sin_ref[...]
    lane = lax.broadcasted_iota(jnp.int32, cos.shape, 1)
    first_half = (lane & (DA_QK // 2)) == 0

    def rope(xx):
        swapped = jnp.where(first_half, pltpu.roll(xx, LANES - DA_QK // 2, 1), pltpu.roll(xx, DA_QK // 2, 1))
        return xx * cos + swapped * sin

    q_off = RW_COLS
    k_off = q_off + DA_QK_WIDTH
    v_off = k_off + DA_QK_WIDTH
    g_off = v_off + DA_V_WIDTH
    q = _dot(hb, w_ref[:, q_off:q_off + DA_QK_WIDTH])
    k = _dot(hb, w_ref[:, k_off:k_off + DA_QK_WIDTH])
    for c in range(0, DA_QK_WIDTH, LANES):
        q_ref[:, c:c + LANES] = (rope(q[:, c:c + LANES]) * (DA_QK ** -0.5)).astype(BF16)
        kr = rope(k[:, c:c + LANES])
        k_ref[:, c:c + LANES] = kr
        kb_ref[:, c:c + LANES] = kr.astype(BF16)
    v = _dot(hb, w_ref[:, v_off:v_off + DA_V_WIDTH])
    v_ref[...] = v
    vb_ref[...] = v.astype(BF16)
    for c in range(0, gate_ref.shape[1], 512):
        gate_ref[:, c:c + 512] = jax.nn.sigmoid(
            _dot(hb, w_ref[:, g_off + c:g_off + c + 512]) + gb_ref[:, c:c + 512])


def _mixproj(x, norm_g, w_in, gate_bias, cos_t, sin_t, rope_blocks, tm=256):
    n, d = x.shape
    in_cols = w_in.shape[1]
    gate_cols = gate_bias.shape[1]
    tm = min(tm, n)
    row = lambda i: (i, 0)
    rope_map = lambda i: (i % rope_blocks, 0)
    outs = [(RW_COLS, F32), (DA_QK_WIDTH, BF16), (DA_QK_WIDTH, F32), (DA_V_WIDTH, F32),
            (DA_QK_WIDTH, BF16), (DA_V_WIDTH, BF16), (gate_cols, F32)]
    return pl.pallas_call(
        _mixproj_kernel,
        grid=(n // tm,),
        in_specs=[pl.BlockSpec((tm, d), row), _const_spec((1, d)), _const_spec((d, in_cols)),
                  _const_spec((1, gate_cols)),
                  pl.BlockSpec((tm, LANES), rope_map), pl.BlockSpec((tm, LANES), rope_map)],
        out_specs=[pl.BlockSpec((tm, w), row) for w, _ in outs],
        out_shape=[jax.ShapeDtypeStruct((n, w), dt) for w, dt in outs],
        compiler_params=_params(("parallel",)),
        name="mixproj",
    )(x, norm_g, w_in, gate_bias, cos_t, sin_t)


def _rwprep_kernel(z_ref, zp_ref, mu_ref, w0_ref, a0_ref, kk_ref, ka_ref, rk_ref,
                   w2_ref, a2_ref, g2_ref, ones_ref,
                   r_ref, dec_ref, k_ref, v_ref, an_ref, bn_ref, g_ref, bonus_ref):
    z = z_ref[...]
    zm = z + (zp_ref[...] - z) * mu_ref[...]
    o1 = RW_WIDTH
    r = zm[:, :o1]
    k = zm[:, o1:2 * o1]
    v = zm[:, 2 * o1:3 * o1]
    lora_in = zm[:, 3 * o1:3 * o1 + LANES]
    gd = zm[:, 3 * o1 + LANES:]
    ones = ones_ref[...]

    x = w0_ref[...] + _dot(jnp.tanh(lora_in).astype(BF16), w2_ref[...])
    neg = -x
    softplus = jnp.maximum(neg, 0.0) + jnp.log1p(jnp.exp(-jnp.abs(neg)))
    dec_ref[...] = jnp.exp(-jnp.exp(-softplus - 0.5))
    a = jax.nn.sigmoid(a0_ref[...] + _dot(lora_in.astype(BF16), a2_ref[...]))
    g_ref[...] = _dot(jax.nn.sigmoid(gd).astype(BF16), g2_ref[...])

    kk = k * kk_ref[...]
    norm = jnp.sqrt(_seg_sum(kk * kk, ones))
    kk = kk / jnp.maximum(norm, 1e-12)
    kmod = k * (1.0 + (a - 1.0) * ka_ref[...])
    r_ref[...] = r
    k_ref[...] = kmod
    v_ref[...] = v
    an_ref[...] = -kk
    bn_ref[...] = kk * a
    bonus_ref[...] = _seg_sum(r * kmod * rk_ref[...], ones) * v


def _rwprep(z, zprev, p, tm=256):
    n = z.shape[0]
    tm = min(tm, n)
    row = lambda i: (i, 0)
    vec = _const_spec((1, RW_WIDTH))
    return pl.pallas_call(
        _rwprep_kernel,
        grid=(n // tm,),
        in_specs=[pl.BlockSpec((tm, RW_COLS), row), pl.BlockSpec((tm, RW_COLS), row),
                  _const_spec((1, RW_COLS)), vec, vec, vec, vec, vec,
                  _const_spec((LANES, RW_WIDTH)), _const_spec((LANES, RW_WIDTH)),
                  _const_spec((GATE_LORA, RW_WIDTH)), _const_spec((RW_WIDTH, RW_WIDTH))],
        out_specs=[pl.BlockSpec((tm, RW_WIDTH), row)] * 8,
        out_shape=[jax.ShapeDtypeStruct((n, RW_WIDTH), F32)] * 8,
        compiler_params=_params(("parallel",)),
        name="rwprep",
    )(z, zprev, p["mu"], p["w0"], p["a0"], p["k_k"], p["k_a"], p["r_k"],
      p["w2"], p["a2"], p["g2"], p["ones"])


def _scan_kernel(a_ref, w_ref, b_ref, k_ref, r_ref, v_ref, s0_ref, spread_ref, y_ref, sf_ref,
                 s_scr, key_scr, val_scr, dots_scr, row_scr, *, tpack):
    tblk = pl.program_id(1)
    tb = y_ref.shape[1]

    @pl.when(tblk == 0)
    def _():
        s_scr[...] = s0_ref[0]

    group_id = lax.broadcasted_iota(jnp.int32, (SCAN_IHI, LANES), 1) // (LANES // SCAN_ILO)

    def prepare(tq, carry):
        a, w, b, k, r, v = (ref[0, tq] for ref in (a_ref, w_ref, b_ref, k_ref, r_ref, v_ref))
        dots = jnp.concatenate([jnp.sum(b * r, axis=0, keepdims=True), jnp.sum(k * r, axis=0, keepdims=True),
                                jnp.zeros((6, LANES), F32)], axis=0)
        x = jnp.concatenate([a, w, b, k, w * r, v, dots], axis=0)
        if tpack > 1:
            hi = x.astype(BF16)
            lo = (x - hi.astype(F32)).astype(BF16)
            x = _dot(jnp.concatenate([hi, lo], axis=1), spread_ref[...])
        for tl in range(tpack):
            t = tq * tpack + tl
            xt = x[:, tl * LANES:(tl + 1) * LANES]
            for qi in range(5):
                key_scr[t, qi] = xt[qi * RW_HEAD:(qi + 1) * RW_HEAD]
            vt = xt[5 * RW_HEAD:6 * RW_HEAD]
            vsel = vt[:SCAN_IHI]
            for g in range(1, SCAN_ILO):
                vsel = jnp.where(group_id == g, vt[g * SCAN_IHI:(g + 1) * SCAN_IHI], vsel)
            val_scr[t] = vsel
            dots_scr[t] = xt[6 * RW_HEAD:6 * RW_HEAD + 2]
        return carry

    n_prep = tb // tpack
    lax.fori_loop(0, n_prep, prepare, 0, unroll=2 if n_prep % 2 == 0 else 1)

    def step(t, carry):
        def group(ih, c):
            s = s_scr[ih]
            sa = jnp.sum(s * key_scr[t, 0], axis=0, keepdims=True)
            row_scr[1, pl.ds(ih, 1), :] = jnp.sum(s * key_scr[t, 4], axis=0, keepdims=True)
            row_scr[0, pl.ds(ih, 1), :] = sa
            s_scr[ih] = s * key_scr[t, 1] + sa * key_scr[t, 2] + val_scr[t, pl.ds(ih, 1), :] * key_scr[t, 3]
            return c

        lax.fori_loop(0, SCAN_IHI, group, 0, unroll=GROUP_UNROLL)
        dots = dots_scr[t]
        y_ref[0, t] = row_scr[1] + row_scr[0] * dots[0:1] + val_scr[t] * dots[1:2]
        return carry

    lax.fori_loop(0, tb, step, 0)

    @pl.when(tblk == pl.num_programs(1) - 1)
    def _():
        sf_ref[0] = s_scr[...]


def _wkv_scan(a, w, b, k, r, v, s0, t, tpack, tb=64):
    n_p = a.shape[0]
    tb = min(tb, t)
    keyspec = pl.BlockSpec((1, tb // tpack, RW_HEAD, LANES), lambda p, i: (p, i, 0, 0))
    valspec = pl.BlockSpec((1, tb, SCAN_IHI, LANES), lambda p, i: (p, i, 0, 0))
    stspec = pl.BlockSpec((1, SCAN_IHI, RW_HEAD, LANES), lambda p, i: (p, 0, 0, 0))
    out_lane = jnp.arange(SCAN_ILO * LANES)
    tok = out_lane // LANES
    seq = (out_lane % (LANES // SCAN_ILO)) // RW_HEADS
    src_lane = seq * (SCAN_ILO * RW_HEADS) + tok * RW_HEADS + out_lane % RW_HEADS
    spread = jnp.tile((jnp.arange(LANES)[:, None] == src_lane[None, :]).astype(BF16), (2, 1))
    return pl.pallas_call(
        functools.partial(_scan_kernel, tpack=tpack),
        grid=(n_p, t // tb),
        in_specs=[keyspec] * 6 + [stspec, _const_spec(spread.shape)],
        out_specs=[valspec, stspec],
        out_shape=[jax.ShapeDtypeStruct((n_p, t, SCAN_IHI, LANES), F32),
                   jax.ShapeDtypeStruct((n_p, SCAN_IHI, RW_HEAD, LANES), F32)],
        scratch_shapes=[pltpu.VMEM((SCAN_IHI, RW_HEAD, LANES), F32), pltpu.VMEM((tb, 5, RW_HEAD, LANES), F32),
                        pltpu.VMEM((tb, SCAN_IHI, LANES), F32), pltpu.VMEM((tb, 2, LANES), F32),
                        pltpu.VMEM((2, SCAN_IHI, LANES), F32)],
        compiler_params=_params(("parallel", "arbitrary")),
        name="wkv_scan",
    )(a, w, b, k, r, v, s0, spread)


def _lanes_kernel(*refs, n, tt):
    rows = SCAN_ILO * RW_HEADS
    th_scr = refs[2 * n]
    for x_ref, o_ref in zip(refs[:n], refs[n:2 * n]):
        for b in range(SCAN_LANE_BATCH):
            x = x_ref[b]
            for h in range(RW_HEADS):
                th_scr[b, pl.ds(h, tt, stride=RW_HEADS), :] = x[:, h * RW_HEAD:(h + 1) * RW_HEAD]
        for tq in range(tt // SCAN_ILO):
            tile = jnp.concatenate([th_scr[b, rows * tq:rows * (tq + 1), :] for b in range(SCAN_LANE_BATCH)],
                                   axis=0)
            o_ref[tq] = tile.T


def _rows_to_lanes(xs, t, tt=64):
    n = len(xs)
    tt = min(tt, t)
    xs = [x.reshape(SCAN_LANE_BATCH, t, RW_WIDTH) for x in xs]
    return pl.pallas_call(
        functools.partial(_lanes_kernel, n=n, tt=tt),
        grid=(t // tt,),
        in_specs=[pl.BlockSpec((SCAN_LANE_BATCH, tt, RW_WIDTH), lambda i: (0, i, 0))] * n,
        out_specs=[pl.BlockSpec((tt // SCAN_ILO, RW_HEAD, LANES), lambda i: (i, 0, 0))] * n,
        out_shape=[jax.ShapeDtypeStruct((t // SCAN_ILO, RW_HEAD, LANES), F32)] * n,
        scratch_shapes=[pltpu.VMEM((SCAN_LANE_BATCH, tt * RW_HEADS, RW_HEAD), F32)],
        compiler_params=_params(("parallel",)),
        name="rows_to_lanes",
    )(*xs)


def _to_copied_lanes(x, bsz, t):
    g = bsz // SCAN_LANE_BATCH
    x = x.reshape(g, SCAN_LANE_BATCH, t, RW_HEADS, RW_HEAD).transpose(0, 2, 4, 1, 3)
    x = jnp.broadcast_to(x[:, :, :, None], (g, t, RW_HEAD, SCAN_ILO, SCAN_LANE_BATCH, RW_HEADS))
    return x.reshape(g, t, RW_HEAD, LANES)


def _from_value_lanes(y, bsz, t):
    g = bsz // SCAN_LANE_BATCH
    y = y.reshape(g, t, SCAN_IHI, SCAN_ILO, SCAN_LANE_BATCH, RW_HEADS).transpose(0, 4, 1, 5, 3, 2)
    return y.reshape(bsz * t, RW_WIDTH)


def _state_to_lanes(s):
    bsz = s.shape[0]
    g = bsz // SCAN_LANE_BATCH
    s = s.reshape(g, SCAN_LANE_BATCH, RW_HEADS, SCAN_ILO, SCAN_IHI, RW_HEAD).transpose(0, 4, 5, 3, 1, 2)
    return s.reshape(g, SCAN_IHI, RW_HEAD, LANES)


def _state_from_lanes(s, bsz):
    g = bsz // SCAN_LANE_BATCH
    s = s.reshape(g, SCAN_IHI, RW_HEAD, SCAN_ILO, SCAN_LANE_BATCH, RW_HEADS).transpose(0, 4, 5, 3, 1, 2)
    return s.reshape(bsz, RW_HEADS, RW_HEAD, RW_HEAD)


def _merge_kernel(x_ref, ys_ref, bonus_ref, g_ref, yb_ref, ga_ref, gb_ref,
                  lnw_ref, lnb_ref, ones_ref, wpa_ref, wpb_ref, wout_ref, o_ref):
    ones = ones_ref[...]
    y = ys_ref[...]
    inv_n = 1.0 / RW_HEAD
    mu = _seg_sum(y, ones) * inv_n
    d = y - mu
    var = _seg_sum(d * d, ones) * inv_n
    ya = d * lax.rsqrt(var + RW_GN_EPS) * lnw_ref[...] + lnb_ref[...]
    ya = (ya + bonus_ref[...]) * g_ref[...]
    merged = (ga_ref[...] * _dot(ya.astype(BF16), wpa_ref[...])
              + gb_ref[...] * _dot(yb_ref[...].astype(BF16), wpb_ref[...]))
    o_ref[...] = x_ref[...] + _dot(merged.astype(BF16), wout_ref[...])


def _merge(x, ys, bonus, g, yb, gates, p, tm=256):
    n, d = x.shape
    tm = min(tm, n)
    row = lambda i: (i, 0)
    half = pl.BlockSpec((tm, RW_WIDTH), row)
    return pl.pallas_call(
        _merge_kernel,
        grid=(n // tm,),
        in_specs=[pl.BlockSpec((tm, d), row), half, half, half, half,
                  pl.BlockSpec((tm, d), lambda i: (i, 0)), pl.BlockSpec((tm, d), lambda i: (i, 1)),
                  _const_spec((1, RW_WIDTH)), _const_spec((1, RW_WIDTH)), _const_spec((RW_WIDTH, RW_WIDTH)),
                  _const_spec((RW_WIDTH, d)), _const_spec((DA_V_WIDTH, d)), _const_spec((d, d))],
        out_specs=pl.BlockSpec((tm, d), row),
        out_shape=jax.ShapeDtypeStruct((n, d), F32),
        compiler_params=_params(("parallel",)),
        name="merge",
    )(x, ys, bonus, g, yb, gates, gates, p["ln_w"], p["ln_b"], p["ones"], p["wpa"], p["wpb"], p["wout"])


def _flash_kernel(lam_ref, q_ref, k_ref, v_ref, sub_ref, o_ref, m_scr, l_scr, acc_scr, *, tq, tk, out_scale):
    qi = pl.program_id(2)
    q = q_ref[...]
    lane = lax.broadcasted_iota(jnp.int32, q.shape, 1)
    zero = jnp.zeros_like(q)
    q2 = jnp.concatenate([jnp.where(lane < DA_QK, q, zero), jnp.where(lane >= DA_QK, q, zero)], axis=0)
    m_scr[...] = jnp.full_like(m_scr, -jnp.inf)
    l_scr[...] = jnp.zeros_like(l_scr)
    acc_scr[...] = jnp.zeros_like(acc_scr)
    nt = (((1,), (1,)), ((), ()))
    reps = tk // LANES
    per_q = tq // tk

    def block(j, diag):
        off = pl.multiple_of(j * tk, tk)
        kb = k_ref[pl.ds(off, tk), :]
        vb = v_ref[pl.ds(off, tk), :]
        s = lax.dot_general(q2, kb, nt, preferred_element_type=F32)
        if diag is not None:
            rr = lax.broadcasted_iota(jnp.int32, s.shape, 0) & (tq - 1)
            cc = lax.broadcasted_iota(jnp.int32, s.shape, 1) + diag * tk
            s = jnp.where(cc <= rr, s, -jnp.inf)
        m_old = m_scr[...]
        m_new = jnp.maximum(m_old, jnp.max(s, axis=-1, keepdims=True))
        alpha = jnp.exp(m_old - m_new)
        p = jnp.exp(s - jnp.concatenate([m_new] * reps, axis=1))
        l_scr[...] = alpha * l_scr[...] + jnp.sum(p, axis=-1, keepdims=True)
        acc_scr[...] = alpha * acc_scr[...] + _dot(p.astype(BF16), vb)
        m_scr[...] = m_new

    def body(j, carry):
        block(j, None)
        return carry

    lax.fori_loop(0, qi * per_q, body, 0)
    for u in range(per_q):
        block(qi * per_q + u, u)

    o = acc_scr[...] / l_scr[...]
    o = o[:tq] - lam_ref[...] * o[tq:]
    o_ref[...] = _rms(o, sub_ref[...]) * out_scale


def _flash(lam_row, qb, kb, vb, subln, bsz, s, out_scale, tq=512, tk=512):
    tq = min(tq, s)
    tk = min(tk, tq)
    q3 = qb.reshape(bsz, s, DA_QK_WIDTH)
    k3 = kb.reshape(bsz, s, DA_QK_WIDTH)
    v3 = vb.reshape(bsz, s, DA_V_WIDTH)
    out = pl.pallas_call(
        functools.partial(_flash_kernel, tq=tq, tk=tk, out_scale=out_scale),
        grid=(bsz, DA_HEADS, s // tq),
        in_specs=[_const_spec((1, LANES)),
                  pl.BlockSpec((None, tq, LANES), lambda b, h, i: (b, i, h)),
                  pl.BlockSpec((None, s, LANES), lambda b, h, i: (b, 0, h)),
                  pl.BlockSpec((None, s, DA_V), lambda b, h, i: (b, 0, h)),
                  pl.BlockSpec((1, DA_V), lambda b, h, i: (0, h))],
        out_specs=pl.BlockSpec((None, tq, DA_V), lambda b, h, i: (b, i, h)),
        out_shape=jax.ShapeDtypeStruct((bsz, s, DA_V_WIDTH), F32),
        scratch_shapes=[pltpu.VMEM((2 * tq, LANES), F32), pltpu.VMEM((2 * tq, LANES), F32),
                        pltpu.VMEM((2 * tq, DA_V), F32)],
        compiler_params=_params(("parallel", "parallel", "arbitrary")),
        name="flash_diff",
    )(lam_row, q3, k3, v3, subln)
    return out.reshape(bsz * s, DA_V_WIDTH)


def _decode_kernel(pt_ref, lam_ref, q_ref, qd_ref, kn_ref, vn_ref, sub_ref, *rest, pages, out_scale):
    k_refs = rest[:pages]
    v_refs = rest[pages:2 * pages]
    o_ref, m_scr, l_scr, acc_scr = rest[2 * pages:]
    j = pl.program_id(1)
    own_head = lax.broadcasted_iota(jnp.int32, acc_scr.shape, 0) >> 1

    @pl.when(j == 0)
    def _():
        m_scr[...] = jnp.sum(q_ref[...] * kn_ref[...], axis=-1, keepdims=True)
        l_scr[...] = jnp.ones_like(l_scr)
        acc_scr[...] = vn_ref[...]

    qd = qd_ref[...]
    s = jnp.concatenate([_dot(qd, k_refs[u][...].astype(BF16)) for u in range(pages)], axis=1)
    m_old = m_scr[...]
    m_new = jnp.maximum(m_old, jnp.max(s, axis=-1, keepdims=True))
    alpha = jnp.exp(m_old - m_new)
    p = jnp.exp(s - m_new)
    l_scr[...] = alpha * l_scr[...] + jnp.sum(p, axis=-1, keepdims=True)
    m_scr[...] = m_new
    pv = jnp.zeros(acc_scr.shape, F32)
    for u in range(pages):
        pu = p[:, u * PAGE_SIZE:(u + 1) * PAGE_SIZE].astype(BF16)
        for hv in range(DA_HEADS):
            vh = v_refs[u][pl.ds(hv, PAGE_SIZE, stride=DA_HEADS), :].astype(BF16)
            pv = pv + jnp.where(own_head == hv, _dot(pu, vh), 0.0)
    acc_scr[...] = alpha * acc_scr[...] + pv

    @pl.when(j == pl.num_programs(1) - 1)
    def _():
        o8 = acc_scr[...] / l_scr[...]
        o = o8 - lam_ref[...] * pltpu.roll(o8, 2 * DA_HEADS - 1, 0)
        o_ref[...] = _rms(o, sub_ref[...]) * out_scale


def _decode_attn(lam_row, q8, k_new8, v_new8, subln8, cache_kt, cache_v, page_table, out_scale, pages=8):
    bsz, n_pages = page_table.shape
    pages = min(pages, n_pages)
    pt = page_table.reshape(-1)
    n_heads = 2 * DA_HEADS

    def page_map(u):
        return lambda b, j, pt_ref: (pt_ref[b * n_pages + j * pages + u], 0, 0)

    q_diag = (q8[:, :, None, :] * jnp.eye(n_heads, dtype=F32)[None, :, :, None]).reshape(bsz, n_heads, -1)
    seq = lambda b, j, pt_ref: (b, 0, 0)
    fixed = lambda b, j, pt_ref: (0, 0)
    in_specs = [pl.BlockSpec((1, LANES), fixed),
                pl.BlockSpec((None, n_heads, DA_QK), seq),
                pl.BlockSpec((None, n_heads, n_heads * DA_QK), seq),
                pl.BlockSpec((None, n_heads, DA_QK), seq),
                pl.BlockSpec((None, n_heads, DA_V), seq),
                pl.BlockSpec((n_heads, DA_V), fixed)]
    in_specs += [pl.BlockSpec((None, n_heads * DA_QK, PAGE_SIZE), page_map(u)) for u in range(pages)]
    in_specs += [pl.BlockSpec((None, PAGE_SIZE * DA_HEADS, DA_V), page_map(u)) for u in range(pages)]
    grid_spec = pltpu.PrefetchScalarGridSpec(
        num_scalar_prefetch=1,
        grid=(bsz, n_pages // pages),
        in_specs=in_specs,
        out_specs=pl.BlockSpec((None, n_heads, DA_V), seq),
        scratch_shapes=[pltpu.VMEM((n_heads, 1), F32), pltpu.VMEM((n_heads, 1), F32),
                        pltpu.VMEM((n_heads, DA_V), F32)])
    return pl.pallas_call(
        functools.partial(_decode_kernel, pages=pages, out_scale=out_scale),
        grid_spec=grid_spec,
        out_shape=jax.ShapeDtypeStruct((bsz, n_heads, DA_V), F32),
        compiler_params=_params(("parallel", "arbitrary")),
        name="decode_attn",
    )(pt, lam_row, q8, q_diag.astype(BF16), k_new8, v_new8, subln8,
      *([cache_kt] * pages), *([cache_v] * pages))


def _rope_tables(pos):
    half = DA_QK // 2
    inv = ROPE_THETA ** (-jnp.arange(half, dtype=F32) / half)
    ang = pos.astype(F32)[:, None] * inv[None, :]
    cos, sin = jnp.cos(ang), jnp.sin(ang)
    reps = LANES // DA_QK
    return jnp.tile(jnp.concatenate([cos, cos], -1), (1, reps)), jnp.tile(jnp.concatenate([-sin, sin], -1), (1, reps))


def _layer_params(l, ffn1_norm, ffn1_up, ffn1_down, mix_norm, w_in, gate_bias, rw_mu, rw_w0, rw_w2, rw_a0,
                  rw_a2, rw_g2, rw_k_k, rw_k_a, rw_r_k, rw_ln_w, rw_ln_b, da_subln, w_proj_a, w_proj_b, w_out,
                  ffn2_norm, ffn2_up, ffn2_down):
    def ffn_w(up, down):
        d, two_ff = up.shape
        ff = two_ff // 2
        tf = 256
        nc = ff // tf
        split = lambda w: w.reshape(d, nc, tf).transpose(1, 0, 2).astype(BF16)
        return split(up[:, :ff]), split(up[:, ff:]), down.reshape(nc, tf, d).astype(BF16)

    row = lambda v: v.reshape(1, -1)
    head_id = jnp.arange(RW_WIDTH) // RW_HEAD
    zeros = jnp.zeros((DECAY_LORA, RW_WIDTH), F32)
    return dict(
        ffn1=(row(ffn1_norm[l]),) + ffn_w(ffn1_up[l], ffn1_down[l]),
        ffn2=(row(ffn2_norm[l]),) + ffn_w(ffn2_up[l], ffn2_down[l]),
        mix_norm=row(mix_norm[l]), w_in=w_in[l].astype(BF16), gate_bias=row(gate_bias[l]),
        mu=row(rw_mu[l]), w0=row(rw_w0[l]), a0=row(rw_a0[l]), k_k=row(rw_k_k[l]), k_a=row(rw_k_a[l]),
        r_k=row(rw_r_k[l]),
        w2=jnp.concatenate([rw_w2[l], zeros], 0).astype(BF16),
        a2=jnp.concatenate([zeros, rw_a2[l]], 0).astype(BF16),
        g2=rw_g2[l].astype(BF16),
        ones=(head_id[:, None] == head_id[None, :]).astype(BF16),
        ln_w=row(rw_ln_w[l]), ln_b=row(rw_ln_b[l]), subln=row(da_subln[l]),
        wpa=w_proj_a[l].astype(BF16), wpb=w_proj_b[l].astype(BF16), wout=w_out[l].astype(BF16))


def _trunk_front(x, p, cos_t, sin_t, rope_blocks, z_last, bsz, t):
    x1 = _ffn(x, *p["ffn1"])
    z, qb, k, v, kb, vb, gates = _mixproj(x1, p["mix_norm"], p["w_in"], p["gate_bias"], cos_t, sin_t, rope_blocks)
    z3 = z.reshape(bsz, t, RW_COLS)
    zprev = jnp.concatenate([z_last[:, None, :], z3[:, :-1]], axis=1).reshape(bsz * t, RW_COLS)
    r, dec, kmod, vv, an, bn, g, bonus = _rwprep(z, zprev, p)
    return x1, z3, qb, k, v, kb, vb, gates, (r, dec, kmod, vv, an, bn), g, bonus


def _rwkv_scan(rw, wkv0, bsz, t):
    r, dec, kmod, vv, an, bn = rw
    ops = [an, dec, bn, kmod, r, vv]
    if bsz == SCAN_LANE_BATCH and t % SCAN_ILO == 0:
        tpack = SCAN_ILO
        ops = [u[None] for u in _rows_to_lanes(ops, t)]
    else:
        tpack = 1
        ops = [_to_copied_lanes(u, bsz, t) for u in ops]
    y, s_fin = _wkv_scan(*ops, _state_to_lanes(wkv0), t, tpack)
    return _from_value_lanes(y, bsz, t), _state_from_lanes(s_fin, bsz)


def kernel(x_prompt, x_sample, cache_k, cache_v, state_wkv, state_shift, page_table, ffn1_norm, ffn1_up, ffn1_down, mix_norm, w_in, gate_bias, rw_mu, rw_w0, rw_w2, rw_a0, rw_a2, rw_g2, rw_k_k, rw_k_a, rw_r_k, rw_ln_w, rw_ln_b, da_lq1, da_lk1, da_lq2, da_lk2, da_subln, w_proj_a, w_proj_b, w_out, ffn2_norm, ffn2_up, ffn2_down, final_norm):
    bp, sp, d = x_prompt.shape
    bs, ts, _ = x_sample.shape
    depth = ffn1_norm.shape[0]
    assert ts == 1 and bp % SCAN_LANE_BATCH == 0 and bs % SCAN_LANE_BATCH == 0
    assert cache_k.shape[2] == PAGE_SIZE == LANES
    n_past = page_table.shape[1] * PAGE_SIZE
    cos_p, sin_p = _rope_tables(jnp.arange(sp, dtype=jnp.int32))
    cos_s, sin_s = _rope_tables(jnp.full((bs,), n_past, dtype=jnp.int32))
    final_row = final_norm.reshape(1, d)

    hp = x_prompt.reshape(bp * sp, d)
    hs = x_sample.reshape(bs * ts, d)
    outs = [[] for _ in range(8)]
    for l in range(depth):
        p = _layer_params(l, ffn1_norm, ffn1_up, ffn1_down, mix_norm, w_in, gate_bias, rw_mu, rw_w0, rw_w2,
                          rw_a0, rw_a2, rw_g2, rw_k_k, rw_k_a, rw_r_k, rw_ln_w, rw_ln_b, da_subln, w_proj_a,
                          w_proj_b, w_out, ffn2_norm, ffn2_up, ffn2_down)
        lam_init = 0.8 - 0.6 * math.exp(-0.3 * l)
        lam = (jnp.exp(jnp.sum(da_lq1[l] * da_lk1[l])) - jnp.exp(jnp.sum(da_lq2[l] * da_lk2[l])) + lam_init)
        lam_row = jnp.full((1, LANES), lam, F32)
        out_scale = 1.0 - lam_init
        last = l == depth - 1

        tm_rope = min(256, bp * sp)
        x1, z3, qb, k, v, kb, vb, gates, rw, g, bonus = _trunk_front(
            hp, p, cos_p, sin_p, sp // min(tm_rope, sp), jnp.zeros((bp, RW_COLS), F32), bp, sp)
        ys, wkv_p = _rwkv_scan(rw, jnp.zeros((bp, RW_HEADS, RW_HEAD, RW_HEAD), F32), bp, sp)
        yb = _flash(lam_row, qb, kb, vb, p["subln"], bp, sp, out_scale)
        x2 = _merge(x1, ys, bonus, g, yb, gates, p)
        hp = _ffn(x2, *p["ffn2"], final_g=final_row if last else None)
        outs[0].append(k.reshape(bp, sp, 2 * DA_HEADS, DA_QK))
        outs[1].append(v.reshape(bp, sp, DA_HEADS, DA_V))
        outs[2].append(wkv_p)
        outs[3].append(z3[:, -1])

        x1, z3, qb, k, v, kb, vb, gates, rw, g, bonus = _trunk_front(
            hs, p, cos_s, sin_s, 1, state_shift[l], bs, ts)
        ys, wkv_s = _rwkv_scan(rw, state_wkv[l], bs, ts)
        q8 = qb.astype(F32).reshape(bs, 2 * DA_HEADS, DA_QK)
        k8 = k.reshape(bs, 2 * DA_HEADS, DA_QK)
        v8 = jnp.repeat(v.reshape(bs, DA_HEADS, DA_V), 2, axis=1)
        sub8 = jnp.repeat(p["subln"].reshape(DA_HEADS, DA_V), 2, axis=0)
        cache_vf = cache_v[l].reshape(cache_v.shape[1], PAGE_SIZE * DA_HEADS, DA_V)
        cache_kt = jnp.transpose(cache_k[l], (0, 2, 3, 1)).reshape(cache_k.shape[1], DA_QK_WIDTH, PAGE_SIZE)
        o8 = _decode_attn(lam_row, q8, k8, v8, sub8, cache_kt, cache_vf, page_table, out_scale)
        yb = o8[:, ::2].reshape(bs, DA_V_WIDTH)
        x2 = _merge(x1, ys, bonus, g, yb, gates, p)
        hs = _ffn(x2, *p["ffn2"], final_g=final_row if last else None)
        outs[4].append(k.reshape(bs, ts, 2 * DA_HEADS, DA_QK))
        outs[5].append(v.reshape(bs, ts, DA_HEADS, DA_V))
        outs[6].append(wkv_s)
        outs[7].append(z3[:, -1])

    st = [jnp.stack(o) for o in outs]
    return (hp.reshape(bp, sp, d), hs.reshape(bs, ts, d), st[0], st[1], st[2], st[3], st[4], st[5], st[6], st[7])
```

```python
import functools
import math

import jax
import jax.numpy as jnp
from jax import lax
from jax.experimental import pallas as pl
from jax.experimental.pallas import tpu as pltpu

F32 = jnp.float32
BF16 = jnp.bfloat16

NORM_EPS = 1e-6
ROPE_THETA = 10000.0
PAGE_SIZE = 128

RW_HEADS = 8
RW_HEAD = 64
RW_WIDTH = RW_HEADS * RW_HEAD
DECAY_LORA = 64
AAA_LORA = 64
GATE_LORA = 128
RW_COLS = 3 * RW_WIDTH + DECAY_LORA + AAA_LORA + GATE_LORA
RW_GN_EPS = 1e-5 * RW_HEAD * RW_HEAD
DA_HEADS = 4
DA_QK = 64
DA_V = 2 * DA_QK
DA_QK_WIDTH = 2 * DA_HEADS * DA_QK
DA_V_WIDTH = DA_HEADS * DA_V

LANES = 128
SCAN_LANE_BATCH = 4
SCAN_ILO = LANES // (SCAN_LANE_BATCH * RW_HEADS)
SCAN_IHI = RW_HEAD // SCAN_ILO
GROUP_UNROLL = 16
VMEM_LIMIT = 56 * 1024 * 1024


def _const_spec(shape):
    nd = len(shape)
    return pl.BlockSpec(shape, lambda *_: (0,) * nd, pipeline_mode=pl.Buffered(1))


def _params(sem):
    return pltpu.CompilerParams(dimension_semantics=sem, vmem_limit_bytes=VMEM_LIMIT)


def _rms(x, g):
    return x * lax.rsqrt(jnp.mean(x * x, axis=-1, keepdims=True) + NORM_EPS) * g


def _dot(a, b):
    return jnp.dot(a, b, preferred_element_type=F32)


def _seg_sum(x, ones_bf):
    hi = x.astype(BF16)
    lo = (x - hi.astype(F32)).astype(BF16)
    return _dot(hi, ones_bf) + _dot(lo, ones_bf)


def _ffn_kernel(x_ref, g_ref, wg_ref, wu_ref, wd_ref, *rest, final):
    if final:
        gf_ref, o_ref, h_scr, acc_scr = rest
    else:
        o_ref, h_scr, acc_scr = rest
    x = x_ref[...]
    h_scr[...] = _rms(x, g_ref[...]).astype(BF16)
    acc_scr[...] = jnp.zeros_like(acc_scr)

    def body(c, carry):
        hb = h_scr[...]
        gate = _dot(hb, wg_ref[c])
        up = _dot(hb, wu_ref[c])
        act = (gate * jax.nn.sigmoid(gate) * up).astype(BF16)
        acc_scr[...] += _dot(act, wd_ref[c])
        return carry

    lax.fori_loop(0, wg_ref.shape[0], body, 0)
    y = x_ref[...] + 0.5 * acc_scr[...]
    if final:
        y = _rms(y, gf_ref[...])
    o_ref[...] = y


def _ffn(x, norm_g, wg, wu, wd, final_g=None, tm=1024):
    n, d = x.shape
    nc, _, tf = wg.shape
    tm = min(tm, n)
    final = final_g is not None
    in_specs = [pl.BlockSpec((tm, d), lambda i: (i, 0)), _const_spec((1, d)),
                _const_spec((nc, d, tf)), _const_spec((nc, d, tf)), _const_spec((nc, tf, d))]
    args = [x, norm_g, wg, wu, wd]
    if final:
        in_specs.append(_const_spec((1, d)))
        args.append(final_g)
    return pl.pallas_call(
        functools.partial(_ffn_kernel, final=final),
        grid=(n // tm,),
        in_specs=in_specs,
        out_specs=pl.BlockSpec((tm, d), lambda i: (i, 0)),
        out_shape=jax.ShapeDtypeStruct((n, d), F32),
        scratch_shapes=[pltpu.VMEM((tm, d), BF16), pltpu.VMEM((tm, d), F32)],
        compiler_params=_params(("parallel",)),
        name="ffn_final" if final else "ffn",
    )(*args)


def _mixproj_kernel(x_ref, g_ref, w_ref, gb_ref, cos_ref, sin_ref,
                    z_ref, q_ref, k_ref, v_ref, kb_ref, vb_ref, gate_ref):
    hb = _rms(x_ref[...], g_ref[...]).astype(BF16)
    for c in range(0, RW_COLS, 256):
        z_ref[:, c:c + 256] = _dot(hb, w_ref[:, c:c + 256])

    cos = cos_ref[...]
    sin = sin_ref[...]
    lane = lax.broadcasted_iota(jnp.int32, cos.shape, 1)
    first_half = (lane & (DA_QK // 2)) == 0

    def rope(xx):
        swapped = jnp.where(first_half, pltpu.roll(xx, LANES - DA_QK // 2, 1), pltpu.roll(xx, DA_QK // 2, 1))
        return xx * cos + swapped * sin

    q_off = RW_COLS
    k_off = q_off + DA_QK_WIDTH
    v_off = k_off + DA_QK_WIDTH
    g_off = v_off + DA_V_WIDTH
    q = _dot(hb, w_ref[:, q_off:q_off + DA_QK_WIDTH])
    k = _dot(hb, w_ref[:, k_off:k_off + DA_QK_WIDTH])
    for c in range(0, DA_QK_WIDTH, LANES):
        q_ref[:, c:c + LANES] = (rope(q[:, c:c + LANES]) * (DA_QK ** -0.5)).astype(BF16)
        kr = rope(k[:, c:c + LANES])
        k_ref[:, c:c + LANES] = kr
        kb_ref[:, c:c + LANES] = kr.astype(BF16)
    v = _dot(hb, w_ref[:, v_off:v_off + DA_V_WIDTH])
    v_ref[...] = v
    vb_ref[...] = v.astype(BF16)
    for c in range(0, gate_ref.shape[1], 512):
        gate_ref[:, c:c + 512] = jax.nn.sigmoid(
            _dot(hb, w_ref[:, g_off + c:g_off + c + 512]) + gb_ref[:, c:c + 512])


def _mixproj(x, norm_g, w_in, gate_bias, cos_t, sin_t, rope_blocks, tm=256):
    n, d = x.shape
    in_cols = w_in.shape[1]
    gate_cols = gate_bias.shape[1]
    tm = min(tm, n)
    row = lambda i: (i, 0)
    rope_map = lambda i: (i % rope_blocks, 0)
    outs = [(RW_COLS, F32), (DA_QK_WIDTH, BF16), (DA_QK_WIDTH, F32), (DA_V_WIDTH, F32),
            (DA_QK_WIDTH, BF16), (DA_V_WIDTH, BF16), (gate_cols, F32)]
    return pl.pallas_call(
        _mixproj_kernel,
        grid=(n // tm,),
        in_specs=[pl.BlockSpec((tm, d), row), _const_spec((1, d)), _const_spec((d, in_cols)),
                  _const_spec((1, gate_cols)),
                  pl.BlockSpec((tm, LANES), rope_map), pl.BlockSpec((tm, LANES), rope_map)],
        out_specs=[pl.BlockSpec((tm, w), row) for w, _ in outs],
        out_shape=[jax.ShapeDtypeStruct((n, w), dt) for w, dt in outs],
        compiler_params=_params(("parallel",)),
        name="mixproj",
    )(x, norm_g, w_in, gate_bias, cos_t, sin_t)


def _rw_vectors(z, zp, mu_ref, w0_ref, a0_ref, kk_ref, ka_ref, rk_ref, w2_ref, a2_ref, g2_ref, ones_ref):
    zm = z + (zp - z) * mu_ref[...]
    o1 = RW_WIDTH
    r = zm[:, :o1]
    k = zm[:, o1:2 * o1]
    v = zm[:, 2 * o1:3 * o1]
    lora_in = zm[:, 3 * o1:3 * o1 + LANES]
    gd = zm[:, 3 * o1 + LANES:]
    ones = ones_ref[...]

    x = w0_ref[...] + _dot(jnp.tanh(lora_in).astype(BF16), w2_ref[...])
    neg = -x
    softplus = jnp.maximum(neg, 0.0) + jnp.log1p(jnp.exp(-jnp.abs(neg)))
    dec = jnp.exp(-jnp.exp(-softplus - 0.5))
    a = jax.nn.sigmoid(a0_ref[...] + _dot(lora_in.astype(BF16), a2_ref[...]))
    g = _dot(jax.nn.sigmoid(gd).astype(BF16), g2_ref[...])

    kk = k * kk_ref[...]
    norm = jnp.sqrt(_seg_sum(kk * kk, ones))
    kk = kk / jnp.maximum(norm, 1e-12)
    kmod = k * (1.0 + (a - 1.0) * ka_ref[...])
    bonus = _seg_sum(r * kmod * rk_ref[...], ones) * v
    return r, dec, kmod, v, -kk, kk * a, g, bonus


def _rwprep_kernel(z_ref, zp_ref, *refs):
    outs = _rw_vectors(z_ref[...], zp_ref[...], *refs[:10])
    for o_ref, val in zip(refs[10:], outs):
        o_ref[...] = val


def _rwprep_lanes_kernel(z_ref, zlast_ref, *refs, tt):
    params = refs[:10]
    lane_refs = refs[10:16]
    g_ref, bonus_ref, carry_scr, th_scr = refs[16:]

    @pl.when(pl.program_id(0) == 0)
    def _():
        carry_scr[...] = zlast_ref[...]

    first_row = lax.broadcasted_iota(jnp.int32, (tt, RW_COLS), 0) == 0
    rows = SCAN_ILO * RW_HEADS
    zs, zps = [], []
    for b in range(SCAN_LANE_BATCH):
        z = z_ref[b]
        zs.append(z)
        zps.append(jnp.where(first_row, carry_scr[pl.ds(b, 1), :], pltpu.roll(z, 1, 0)))
        carry_scr[pl.ds(b, 1), :] = z[tt - 1:tt, :]
    r, dec, kmod, v, an, bn, g, bonus = _rw_vectors(jnp.concatenate(zs, axis=0), jnp.concatenate(zps, axis=0), *params)
    for b in range(SCAN_LANE_BATCH):
        seq_rows = slice(b * tt, (b + 1) * tt)
        g_ref[b] = g[seq_rows]
        bonus_ref[b] = bonus[seq_rows]
        for idx, val in enumerate((an, dec, bn, kmod, r, v)):
            for h in range(RW_HEADS):
                th_scr[idx, b, pl.ds(h, tt, stride=RW_HEADS), :] = val[seq_rows, h * RW_HEAD:(h + 1) * RW_HEAD]
    for idx, o_ref in enumerate(lane_refs):
        for tq in range(tt // SCAN_ILO):
            tile = jnp.concatenate([th_scr[idx, b, rows * tq:rows * (tq + 1), :] for b in range(SCAN_LANE_BATCH)],
                                   axis=0)
            o_ref[tq] = tile.T


def _rwprep_lanes(z3, z_last, p, tt=64):
    bsz, t, _ = z3.shape
    tt = min(tt, t)
    vec = _const_spec((1, RW_WIDTH))
    lanes_spec = pl.BlockSpec((tt // SCAN_ILO, RW_HEAD, LANES), lambda i: (i, 0, 0))
    rows_spec = pl.BlockSpec((bsz, tt, RW_WIDTH), lambda i: (0, i, 0))
    outs = pl.pallas_call(
        functools.partial(_rwprep_lanes_kernel, tt=tt),
        grid=(t // tt,),
        in_specs=[pl.BlockSpec((bsz, tt, RW_COLS), lambda i: (0, i, 0)), _const_spec((bsz, RW_COLS)),
                  _const_spec((1, RW_COLS)), vec, vec, vec, vec, vec,
                  _const_spec((LANES, RW_WIDTH)), _const_spec((LANES, RW_WIDTH)),
                  _const_spec((GATE_LORA, RW_WIDTH)), _const_spec((RW_WIDTH, RW_WIDTH))],
        out_specs=[lanes_spec] * 6 + [rows_spec] * 2,
        out_shape=[jax.ShapeDtypeStruct((t // SCAN_ILO, RW_HEAD, LANES), F32)] * 6
        + [jax.ShapeDtypeStruct((bsz, t, RW_WIDTH), F32)] * 2,
        scratch_shapes=[pltpu.VMEM((bsz, RW_COLS), F32),
                        pltpu.VMEM((6, bsz, tt * RW_HEADS, RW_HEAD), F32)],
        compiler_params=_params(("arbitrary",)),
        name="rwprep_lanes",
    )(z3, z_last, p["mu"], p["w0"], p["a0"], p["k_k"], p["k_a"], p["r_k"], p["w2"], p["a2"], p["g2"], p["ones"])
    return outs[:6], outs[6].reshape(bsz * t, RW_WIDTH), outs[7].reshape(bsz * t, RW_WIDTH)


def _rwprep(z, zprev, p, tm=256):
    n = z.shape[0]
    tm = min(tm, n)
    row = lambda i: (i, 0)
    vec = _const_spec((1, RW_WIDTH))
    return pl.pallas_call(
        _rwprep_kernel,
        grid=(n // tm,),
        in_specs=[pl.BlockSpec((tm, RW_COLS), row), pl.BlockSpec((tm, RW_COLS), row),
                  _const_spec((1, RW_COLS)), vec, vec, vec, vec, vec,
                  _const_spec((LANES, RW_WIDTH)), _const_spec((LANES, RW_WIDTH)),
                  _const_spec((GATE_LORA, RW_WIDTH)), _const_spec((RW_WIDTH, RW_WIDTH))],
        out_specs=[pl.BlockSpec((tm, RW_WIDTH), row)] * 8,
        out_shape=[jax.ShapeDtypeStruct((n, RW_WIDTH), F32)] * 8,
        compiler_params=_params(("parallel",)),
        name="rwprep",
    )(z, zprev, p["mu"], p["w0"], p["a0"], p["k_k"], p["k_a"], p["r_k"],
      p["w2"], p["a2"], p["g2"], p["ones"])


def _scan_kernel(a_ref, w_ref, b_ref, k_ref, r_ref, v_ref, s0_ref, spread_ref, y_ref, sf_ref,
                 s_scr, key_scr, val_scr, dots_scr, row_scr, *, tpack):
    tblk = pl.program_id(1)
    tb = y_ref.shape[1]

    @pl.when(tblk == 0)
    def _():
        s_scr[...] = s0_ref[0]

    group_id = lax.broadcasted_iota(jnp.int32, (SCAN_IHI, LANES), 1) // (LANES // SCAN_ILO)

    def prepare(tq, carry):
        a, w, b, k, r, v = (ref[0, tq] for ref in (a_ref, w_ref, b_ref, k_ref, r_ref, v_ref))
        dots = jnp.concatenate([jnp.sum(b * r, axis=0, keepdims=True), jnp.sum(k * r, axis=0, keepdims=True),
                                jnp.zeros((6, LANES), F32)], axis=0)
        x = jnp.concatenate([a, w, b, k, w * r, v, dots], axis=0)
        if tpack > 1:
            hi = x.astype(BF16)
            lo = (x - hi.astype(F32)).astype(BF16)
            x = _dot(jnp.concatenate([hi, lo], axis=1), spread_ref[...])
        for tl in range(tpack):
            t = tq * tpack + tl
            xt = x[:, tl * LANES:(tl + 1) * LANES]
            for qi in range(5):
                key_scr[t, qi] = xt[qi * RW_HEAD:(qi + 1) * RW_HEAD]
            vt = xt[5 * RW_HEAD:6 * RW_HEAD]
            vsel = vt[:SCAN_IHI]
            for g in range(1, SCAN_ILO):
                vsel = jnp.where(group_id == g, vt[g * SCAN_IHI:(g + 1) * SCAN_IHI], vsel)
            val_scr[t] = vsel
            dots_scr[t] = xt[6 * RW_HEAD:6 * RW_HEAD + 2]
        return carry

    n_prep = tb // tpack

    def step(t):
        def group(ih, c):
            s = s_scr[ih]
            sa = jnp.sum(s * key_scr[t, 0], axis=0, keepdims=True)
            row_scr[1, pl.ds(ih, 1), :] = jnp.sum(s * key_scr[t, 4], axis=0, keepdims=True)
            row_scr[0, pl.ds(ih, 1), :] = sa
            s_scr[ih] = s * key_scr[t, 1] + sa * key_scr[t, 2] + val_scr[t, pl.ds(ih, 1), :] * key_scr[t, 3]
            return c

        lax.fori_loop(0, SCAN_IHI, group, 0, unroll=GROUP_UNROLL)
        dots = dots_scr[t]
        y_ref[0, t] = row_scr[1] + row_scr[0] * dots[0:1] + val_scr[t] * dots[1:2]

    def body(t, carry):
        step(t)
        return carry

    lax.fori_loop(0, n_prep, prepare, 0, unroll=2 if n_prep % 2 == 0 else 1)
    lax.fori_loop(0, tb, body, 0)

    @pl.when(tblk == pl.num_programs(1) - 1)
    def _():
        sf_ref[0] = s_scr[...]


def _wkv_scan(a, w, b, k, r, v, s0, t, tpack, tb=64):
    n_p = a.shape[0]
    tb = min(tb, t)
    keyspec = pl.BlockSpec((1, tb // tpack, RW_HEAD, LANES), lambda p, i: (p, i, 0, 0))
    valspec = pl.BlockSpec((1, tb, SCAN_IHI, LANES), lambda p, i: (p, i, 0, 0))
    stspec = pl.BlockSpec((1, SCAN_IHI, RW_HEAD, LANES), lambda p, i: (p, 0, 0, 0))
    out_lane = jnp.arange(SCAN_ILO * LANES)
    tok = out_lane // LANES
    seq = (out_lane % (LANES // SCAN_ILO)) // RW_HEADS
    src_lane = seq * (SCAN_ILO * RW_HEADS) + tok * RW_HEADS + out_lane % RW_HEADS
    spread = jnp.tile((jnp.arange(LANES)[:, None] == src_lane[None, :]).astype(BF16), (2, 1))
    return pl.pallas_call(
        functools.partial(_scan_kernel, tpack=tpack),
        grid=(n_p, t // tb),
        in_specs=[keyspec] * 6 + [stspec, _const_spec(spread.shape)],
        out_specs=[valspec, stspec],
        out_shape=[jax.ShapeDtypeStruct((n_p, t, SCAN_IHI, LANES), F32),
                   jax.ShapeDtypeStruct((n_p, SCAN_IHI, RW_HEAD, LANES), F32)],
        scratch_shapes=[pltpu.VMEM((SCAN_IHI, RW_HEAD, LANES), F32), pltpu.VMEM((tb, 5, RW_HEAD, LANES), F32),
                        pltpu.VMEM((tb, SCAN_IHI, LANES), F32), pltpu.VMEM((tb, 2, LANES), F32),
                        pltpu.VMEM((2, SCAN_IHI, LANES), F32)],
        compiler_params=_params(("parallel", "arbitrary")),
        name="wkv_scan",
    )(a, w, b, k, r, v, s0, spread)


def _to_copied_lanes(x, bsz, t):
    g = bsz // SCAN_LANE_BATCH
    x = x.reshape(g, SCAN_LANE_BATCH, t, RW_HEADS, RW_HEAD).transpose(0, 2, 4, 1, 3)
    x = jnp.broadcast_to(x[:, :, :, None], (g, t, RW_HEAD, SCAN_ILO, SCAN_LANE_BATCH, RW_HEADS))
    return x.reshape(g, t, RW_HEAD, LANES)


def _from_value_lanes(y, bsz, t):
    g = bsz // SCAN_LANE_BATCH
    y = y.reshape(g, t, SCAN_IHI, SCAN_ILO, SCAN_LANE_BATCH, RW_HEADS).transpose(0, 4, 1, 5, 3, 2)
    return y.reshape(bsz * t, RW_WIDTH)


def _state_to_lanes(s):
    bsz = s.shape[0]
    g = bsz // SCAN_LANE_BATCH
    s = s.reshape(g, SCAN_LANE_BATCH, RW_HEADS, SCAN_ILO, SCAN_IHI, RW_HEAD).transpose(0, 4, 5, 3, 1, 2)
    return s.reshape(g, SCAN_IHI, RW_HEAD, LANES)


def _state_from_lanes(s, bsz):
    g = bsz // SCAN_LANE_BATCH
    s = s.reshape(g, SCAN_IHI, RW_HEAD, SCAN_ILO, SCAN_LANE_BATCH, RW_HEADS).transpose(0, 4, 5, 3, 1, 2)
    return s.reshape(bsz, RW_HEADS, RW_HEAD, RW_HEAD)


def _merge_kernel(x_ref, ys_ref, bonus_ref, g_ref, yb_ref, ga_ref, gb_ref,
                  lnw_ref, lnb_ref, ones_ref, wpa_ref, wpb_ref, wout_ref, o_ref):
    ones = ones_ref[...]
    y = ys_ref[...]
    inv_n = 1.0 / RW_HEAD
    mu = _seg_sum(y, ones) * inv_n
    d = y - mu
    var = _seg_sum(d * d, ones) * inv_n
    ya = d * lax.rsqrt(var + RW_GN_EPS) * lnw_ref[...] + lnb_ref[...]
    ya = (ya + bonus_ref[...]) * g_ref[...]
    merged = (ga_ref[...] * _dot(ya.astype(BF16), wpa_ref[...])
              + gb_ref[...] * _dot(yb_ref[...].astype(BF16), wpb_ref[...]))
    o_ref[...] = x_ref[...] + _dot(merged.astype(BF16), wout_ref[...])


def _merge(x, ys, bonus, g, yb, gates, p, tm=256):
    n, d = x.shape
    tm = min(tm, n)
    row = lambda i: (i, 0)
    half = pl.BlockSpec((tm, RW_WIDTH), row)
    return pl.pallas_call(
        _merge_kernel,
        grid=(n // tm,),
        in_specs=[pl.BlockSpec((tm, d), row), half, half, half, half,
                  pl.BlockSpec((tm, d), lambda i: (i, 0)), pl.BlockSpec((tm, d), lambda i: (i, 1)),
                  _const_spec((1, RW_WIDTH)), _const_spec((1, RW_WIDTH)), _const_spec((RW_WIDTH, RW_WIDTH)),
                  _const_spec((RW_WIDTH, d)), _const_spec((DA_V_WIDTH, d)), _const_spec((d, d))],
        out_specs=pl.BlockSpec((tm, d), row),
        out_shape=jax.ShapeDtypeStruct((n, d), F32),
        compiler_params=_params(("parallel",)),
        name="merge",
    )(x, ys, bonus, g, yb, gates, gates, p["ln_w"], p["ln_b"], p["ones"], p["wpa"], p["wpb"], p["wout"])


def _flash_kernel(lam_ref, q_ref, k_ref, v_ref, sub_ref, o_ref, m_scr, l_scr, acc_scr, *, tq, out_scale):
    qi = pl.program_id(2)
    q = q_ref[...]
    lane = lax.broadcasted_iota(jnp.int32, q.shape, 1)
    zero = jnp.zeros_like(q)
    q2 = jnp.concatenate([jnp.where(lane < DA_QK, q, zero), jnp.where(lane >= DA_QK, q, zero)], axis=0)
    m_scr[...] = jnp.full_like(m_scr, -jnp.inf)
    l_scr[...] = jnp.zeros_like(l_scr)
    acc_scr[...] = jnp.zeros_like(acc_scr)
    nt = (((1,), (1,)), ((), ()))
    reps = tq // LANES

    def block(j, diagonal):
        off = pl.multiple_of(j * tq, tq)
        kb = k_ref[pl.ds(off, tq), :]
        vb = v_ref[pl.ds(off, tq), :]
        s = lax.dot_general(q2, kb, nt, preferred_element_type=F32)
        if diagonal:
            rr = lax.broadcasted_iota(jnp.int32, s.shape, 0) & (tq - 1)
            cc = lax.broadcasted_iota(jnp.int32, s.shape, 1)
            s = jnp.where(cc <= rr, s, -jnp.inf)
        m_old = m_scr[...]
        m_new = jnp.maximum(m_old, jnp.max(s, axis=-1, keepdims=True))
        alpha = jnp.exp(m_old - m_new)
        p = jnp.exp(s - jnp.concatenate([m_new] * reps, axis=1))
        l_scr[...] = alpha * l_scr[...] + jnp.sum(p, axis=-1, keepdims=True)
        acc_scr[...] = alpha * acc_scr[...] + _dot(p.astype(BF16), vb)
        m_scr[...] = m_new

    def body(j, carry):
        block(j, False)
        return carry

    lax.fori_loop(0, qi, body, 0)
    block(qi, True)

    o = acc_scr[...] / l_scr[...]
    o = o[:tq] - lam_ref[...] * o[tq:]
    o_ref[...] = _rms(o, sub_ref[...]) * out_scale


def _flash(lam_row, qb, kb, vb, subln, bsz, s, out_scale, tq=512):
    tq = min(tq, s)
    q3 = qb.reshape(bsz, s, DA_QK_WIDTH)
    k3 = kb.reshape(bsz, s, DA_QK_WIDTH)
    v3 = vb.reshape(bsz, s, DA_V_WIDTH)
    out = pl.pallas_call(
        functools.partial(_flash_kernel, tq=tq, out_scale=out_scale),
        grid=(bsz, DA_HEADS, s // tq),
        in_specs=[_const_spec((1, LANES)),
                  pl.BlockSpec((None, tq, LANES), lambda b, h, i: (b, i, h)),
                  pl.BlockSpec((None, s, LANES), lambda b, h, i: (b, 0, h)),
                  pl.BlockSpec((None, s, DA_V), lambda b, h, i: (b, 0, h)),
                  pl.BlockSpec((1, DA_V), lambda b, h, i: (0, h))],
        out_specs=pl.BlockSpec((None, tq, DA_V), lambda b, h, i: (b, i, h)),
        out_shape=jax.ShapeDtypeStruct((bsz, s, DA_V_WIDTH), F32),
        scratch_shapes=[pltpu.VMEM((2 * tq, LANES), F32), pltpu.VMEM((2 * tq, LANES), F32),
                        pltpu.VMEM((2 * tq, DA_V), F32)],
        compiler_params=_params(("parallel", "parallel", "arbitrary")),
        name="flash_diff",
    )(lam_row, q3, k3, v3, subln)
    return out.reshape(bsz * s, DA_V_WIDTH)


def _decode_kernel(pt_ref, lam_ref, q_ref, qd_ref, kn_ref, vn_ref, sub_ref, *rest, pages, out_scale):
    k_refs = rest[:pages]
    v_refs = rest[pages:2 * pages]
    o_ref, m_scr, l_scr, acc_scr = rest[2 * pages:]
    j = pl.program_id(1)
    own_head = lax.broadcasted_iota(jnp.int32, acc_scr.shape, 0) >> 1

    @pl.when(j == 0)
    def _():
        m_scr[...] = jnp.sum(q_ref[...] * kn_ref[...], axis=-1, keepdims=True)
        l_scr[...] = jnp.ones_like(l_scr)
        acc_scr[...] = vn_ref[...]

    qd = qd_ref[...]
    s = jnp.concatenate([_dot(qd, k_refs[u][...].astype(BF16)) for u in range(pages)], axis=1)
    m_old = m_scr[...]
    m_new = jnp.maximum(m_old, jnp.max(s, axis=-1, keepdims=True))
    alpha = jnp.exp(m_old - m_new)
    p = jnp.exp(s - m_new)
    l_scr[...] = alpha * l_scr[...] + jnp.sum(p, axis=-1, keepdims=True)
    m_scr[...] = m_new
    pv = jnp.zeros(acc_scr.shape, F32)
    for u in range(pages):
        pu = p[:, u * PAGE_SIZE:(u + 1) * PAGE_SIZE].astype(BF16)
        for hv in range(DA_HEADS):
            vh = v_refs[u][pl.ds(hv, PAGE_SIZE, stride=DA_HEADS), :].astype(BF16)
            pv = pv + jnp.where(own_head == hv, _dot(pu, vh), 0.0)
    acc_scr[...] = alpha * acc_scr[...] + pv

    @pl.when(j == pl.num_programs(1) - 1)
    def _():
        o8 = acc_scr[...] / l_scr[...]
        o = o8 - lam_ref[...] * pltpu.roll(o8, 2 * DA_HEADS - 1, 0)
        o_ref[...] = _rms(o, sub_ref[...]) * out_scale


def _decode_attn(lam_row, q8, k_new8, v_new8, subln8, cache_kt, cache_v, page_table, out_scale, pages=16):
    bsz, n_pages = page_table.shape
    pages = min(pages, n_pages)
    pt = page_table.reshape(-1)
    n_heads = 2 * DA_HEADS

    def page_map(u):
        return lambda b, j, pt_ref: (pt_ref[b * n_pages + j * pages + u], 0, 0)

    q_diag = (q8[:, :, None, :] * jnp.eye(n_heads, dtype=F32)[None, :, :, None]).reshape(bsz, n_heads, -1)
    seq = lambda b, j, pt_ref: (b, 0, 0)
    fixed = lambda b, j, pt_ref: (0, 0)
    in_specs = [pl.BlockSpec((1, LANES), fixed),
                pl.BlockSpec((None, n_heads, DA_QK), seq),
                pl.BlockSpec((None, n_heads, n_heads * DA_QK), seq),
                pl.BlockSpec((None, n_heads, DA_QK), seq),
                pl.BlockSpec((None, n_heads, DA_V), seq),
                pl.BlockSpec((n_heads, DA_V), fixed)]
    in_specs += [pl.BlockSpec((None, n_heads * DA_QK, PAGE_SIZE), page_map(u)) for u in range(pages)]
    in_specs += [pl.BlockSpec((None, PAGE_SIZE * DA_HEADS, DA_V), page_map(u)) for u in range(pages)]
    grid_spec = pltpu.PrefetchScalarGridSpec(
        num_scalar_prefetch=1,
        grid=(bsz, n_pages // pages),
        in_specs=in_specs,
        out_specs=pl.BlockSpec((None, n_heads, DA_V), seq),
        scratch_shapes=[pltpu.VMEM((n_heads, 1), F32), pltpu.VMEM((n_heads, 1), F32),
                        pltpu.VMEM((n_heads, DA_V), F32)])
    return pl.pallas_call(
        functools.partial(_decode_kernel, pages=pages, out_scale=out_scale),
        grid_spec=grid_spec,
        out_shape=jax.ShapeDtypeStruct((bsz, n_heads, DA_V), F32),
        compiler_params=_params(("parallel", "arbitrary")),
        name="decode_attn",
    )(pt, lam_row, q8, q_diag.astype(BF16), k_new8, v_new8, subln8,
      *([cache_kt] * pages), *([cache_v] * pages))


def _rope_tables(pos):
    half = DA_QK // 2
    inv = ROPE_THETA ** (-jnp.arange(half, dtype=F32) / half)
    ang = pos.astype(F32)[:, None] * inv[None, :]
    cos, sin = jnp.cos(ang), jnp.sin(ang)
    reps = LANES // DA_QK
    return jnp.tile(jnp.concatenate([cos, cos], -1), (1, reps)), jnp.tile(jnp.concatenate([-sin, sin], -1), (1, reps))


def _layer_params(l, ffn1_norm, ffn1_up, ffn1_down, mix_norm, w_in, gate_bias, rw_mu, rw_w0, rw_w2, rw_a0,
                  rw_a2, rw_g2, rw_k_k, rw_k_a, rw_r_k, rw_ln_w, rw_ln_b, da_subln, w_proj_a, w_proj_b, w_out,
                  ffn2_norm, ffn2_up, ffn2_down):
    def ffn_w(up, down):
        d, two_ff = up.shape
        ff = two_ff // 2
        tf = 256
        nc = ff // tf
        split = lambda w: w.reshape(d, nc, tf).transpose(1, 0, 2).astype(BF16)
        return split(up[:, :ff]), split(up[:, ff:]), down.reshape(nc, tf, d).astype(BF16)

    row = lambda v: v.reshape(1, -1)
    head_id = jnp.arange(RW_WIDTH) // RW_HEAD
    zeros = jnp.zeros((DECAY_LORA, RW_WIDTH), F32)
    return dict(
        ffn1=(row(ffn1_norm[l]),) + ffn_w(ffn1_up[l], ffn1_down[l]),
        ffn2=(row(ffn2_norm[l]),) + ffn_w(ffn2_up[l], ffn2_down[l]),
        mix_norm=row(mix_norm[l]), w_in=w_in[l].astype(BF16), gate_bias=row(gate_bias[l]),
        mu=row(rw_mu[l]), w0=row(rw_w0[l]), a0=row(rw_a0[l]), k_k=row(rw_k_k[l]), k_a=row(rw_k_a[l]),
        r_k=row(rw_r_k[l]),
        w2=jnp.concatenate([rw_w2[l], zeros], 0).astype(BF16),
        a2=jnp.concatenate([zeros, rw_a2[l]], 0).astype(BF16),
        g2=rw_g2[l].astype(BF16),
        ones=(head_id[:, None] == head_id[None, :]).astype(BF16),
        ln_w=row(rw_ln_w[l]), ln_b=row(rw_ln_b[l]), subln=row(da_subln[l]),
        wpa=w_proj_a[l].astype(BF16), wpb=w_proj_b[l].astype(BF16), wout=w_out[l].astype(BF16))


def _trunk_front(x, p, cos_t, sin_t, rope_blocks, z_last, bsz, t):
    x1 = _ffn(x, *p["ffn1"])
    z, qb, k, v, kb, vb, gates = _mixproj(x1, p["mix_norm"], p["w_in"], p["gate_bias"], cos_t, sin_t, rope_blocks)
    z3 = z.reshape(bsz, t, RW_COLS)
    if bsz == SCAN_LANE_BATCH and t % SCAN_ILO == 0:
        tpack = SCAN_ILO
        ops, g, bonus = _rwprep_lanes(z3, z_last, p)
        ops = [u[None] for u in ops]
    else:
        assert t == 1
        tpack = 1
        r, dec, kmod, vv, an, bn, g, bonus = _rwprep(z, z_last, p)
        ops = [_to_copied_lanes(u, bsz, t) for u in (an, dec, bn, kmod, r, vv)]
    return x1, z3, qb, k, v, kb, vb, gates, (ops, tpack), g, bonus


def _rwkv_scan(rw, wkv0, bsz, t):
    ops, tpack = rw
    y, s_fin = _wkv_scan(*ops, _state_to_lanes(wkv0), t, tpack)
    return _from_value_lanes(y, bsz, t), _state_from_lanes(s_fin, bsz)


def kernel(x_prompt, x_sample, cache_k, cache_v, state_wkv, state_shift, page_table, ffn1_norm, ffn1_up, ffn1_down, mix_norm, w_in, gate_bias, rw_mu, rw_w0, rw_w2, rw_a0, rw_a2, rw_g2, rw_k_k, rw_k_a, rw_r_k, rw_ln_w, rw_ln_b, da_lq1, da_lk1, da_lq2, da_lk2, da_subln, w_proj_a, w_proj_b, w_out, ffn2_norm, ffn2_up, ffn2_down, final_norm):
    bp, sp, d = x_prompt.shape
    bs, ts, _ = x_sample.shape
    depth = ffn1_norm.shape[0]
    assert ts == 1 and bp % SCAN_LANE_BATCH == 0 and bs % SCAN_LANE_BATCH == 0
    assert cache_k.shape[2] == PAGE_SIZE == LANES
    n_past = page_table.shape[1] * PAGE_SIZE
    cos_p, sin_p = _rope_tables(jnp.arange(sp, dtype=jnp.int32))
    cos_s, sin_s = _rope_tables(jnp.full((bs,), n_past, dtype=jnp.int32))
    final_row = final_norm.reshape(1, d)

    hp = x_prompt.reshape(bp * sp, d)
    hs = x_sample.reshape(bs * ts, d)
    outs = [[] for _ in range(8)]
    for l in range(depth):
        p = _layer_params(l, ffn1_norm, ffn1_up, ffn1_down, mix_norm, w_in, gate_bias, rw_mu, rw_w0, rw_w2,
                          rw_a0, rw_a2, rw_g2, rw_k_k, rw_k_a, rw_r_k, rw_ln_w, rw_ln_b, da_subln, w_proj_a,
                          w_proj_b, w_out, ffn2_norm, ffn2_up, ffn2_down)
        lam_init = 0.8 - 0.6 * math.exp(-0.3 * l)
        lam = (jnp.exp(jnp.sum(da_lq1[l] * da_lk1[l])) - jnp.exp(jnp.sum(da_lq2[l] * da_lk2[l])) + lam_init)
        lam_row = jnp.full((1, LANES), lam, F32)
        out_scale = 1.0 - lam_init
        last = l == depth - 1

        tm_rope = min(256, bp * sp)
        x1, z3, qb, k, v, kb, vb, gates, rw, g, bonus = _trunk_front(
            hp, p, cos_p, sin_p, sp // min(tm_rope, sp), jnp.zeros((bp, RW_COLS), F32), bp, sp)
        ys, wkv_p = _rwkv_scan(rw, jnp.zeros((bp, RW_HEADS, RW_HEAD, RW_HEAD), F32), bp, sp)
        yb = _flash(lam_row, qb, kb, vb, p["subln"], bp, sp, out_scale)
        x2 = _merge(x1, ys, bonus, g, yb, gates, p)
        hp = _ffn(x2, *p["ffn2"], final_g=final_row if last else None)
        outs[0].append(k.reshape(bp, sp, 2 * DA_HEADS, DA_QK))
        outs[1].append(v.reshape(bp, sp, DA_HEADS, DA_V))
        outs[2].append(wkv_p)
        outs[3].append(z3[:, -1])

        x1, z3, qb, k, v, kb, vb, gates, rw, g, bonus = _trunk_front(
            hs, p, cos_s, sin_s, 1, state_shift[l], bs, ts)
        ys, wkv_s = _rwkv_scan(rw, state_wkv[l], bs, ts)
        q8 = qb.astype(F32).reshape(bs, 2 * DA_HEADS, DA_QK)
        k8 = k.reshape(bs, 2 * DA_HEADS, DA_QK)
        v8 = jnp.repeat(v.reshape(bs, DA_HEADS, DA_V), 2, axis=1)
        sub8 = jnp.repeat(p["subln"].reshape(DA_HEADS, DA_V), 2, axis=0)
        cache_vf = cache_v[l].reshape(cache_v.shape[1], PAGE_SIZE * DA_HEADS, DA_V)
        cache_kt = jnp.transpose(cache_k[l], (0, 2, 3, 1)).reshape(cache_k.shape[1], DA_QK_WIDTH, PAGE_SIZE)
        o8 = _decode_attn(lam_row, q8, k8, v8, sub8, cache_kt, cache_vf, page_table, out_scale)
        yb = o8[:, ::2].reshape(bs, DA_V_WIDTH)
        x2 = _merge(x1, ys, bonus, g, yb, gates, p)
        hs = _ffn(x2, *p["ffn2"], final_g=final_row if last else None)
        outs[4].append(k.reshape(bs, ts, 2 * DA_HEADS, DA_QK))
        outs[5].append(v.reshape(bs, ts, DA_HEADS, DA_V))
        outs[6].append(wkv_s)
        outs[7].append(z3[:, -1])

    st = [jnp.stack(o) for o in outs]
    return (hp.reshape(bp, sp, d), hs.reshape(bs, ts, d), st[0], st[1], st[2], st[3], st[4], st[5], st[6], st[7])
```

```python
import functools
import math

import jax
import jax.numpy as jnp
from jax import lax
from jax.experimental import pallas as pl
from jax.experimental.pallas import tpu as pltpu

F32 = jnp.float32
BF16 = jnp.bfloat16

NORM_EPS = 1e-6
ROPE_THETA = 10000.0
PAGE_SIZE = 128

RW_HEADS = 8
RW_HEAD = 64
RW_WIDTH = RW_HEADS * RW_HEAD
DECAY_LORA = 64
AAA_LORA = 64
GATE_LORA = 128
RW_COLS = 3 * RW_WIDTH + DECAY_LORA + AAA_LORA + GATE_LORA
RW_GN_EPS = 1e-5 * RW_HEAD * RW_HEAD
DA_HEADS = 4
DA_QK = 64
DA_V = 2 * DA_QK
DA_QK_WIDTH = 2 * DA_HEADS * DA_QK
DA_V_WIDTH = DA_HEADS * DA_V
Q_SCALE = DA_QK ** -0.5 * math.log2(math.e)

LANES = 128
SCAN_LANE_BATCH = 4
SCAN_ILO = LANES // (SCAN_LANE_BATCH * RW_HEADS)
SCAN_IHI = RW_HEAD // SCAN_ILO
GROUP_UNROLL = 16
VMEM_LIMIT = 56 * 1024 * 1024


def _const_spec(shape):
    nd = len(shape)
    return pl.BlockSpec(shape, lambda *_: (0,) * nd, pipeline_mode=pl.Buffered(1))


def _params(sem):
    return pltpu.CompilerParams(dimension_semantics=sem, vmem_limit_bytes=VMEM_LIMIT)


def _rms(x, g):
    return x * lax.rsqrt(jnp.mean(x * x, axis=-1, keepdims=True) + NORM_EPS) * g


def _dot(a, b):
    return jnp.dot(a, b, preferred_element_type=F32)


def _seg_sum(x, ones_bf):
    hi = x.astype(BF16)
    lo = (x - hi.astype(F32)).astype(BF16)
    return _dot(hi, ones_bf) + _dot(lo, ones_bf)


def _ffn_kernel(x_ref, g_ref, wg_ref, wu_ref, wd_ref, *rest, final):
    if final:
        gf_ref, o_ref, h_scr, acc_scr = rest
    else:
        o_ref, h_scr, acc_scr = rest
    x = x_ref[...]
    h_scr[...] = _rms(x, g_ref[...]).astype(BF16)
    acc_scr[...] = jnp.zeros_like(acc_scr)

    def body(c, carry):
        hb = h_scr[...]
        gate = _dot(hb, wg_ref[c])
        up = _dot(hb, wu_ref[c])
        act = (gate * jax.nn.sigmoid(gate) * up).astype(BF16)
        acc_scr[...] += _dot(act, wd_ref[c])
        return carry

    lax.fori_loop(0, wg_ref.shape[0], body, 0)
    y = x_ref[...] + 0.5 * acc_scr[...]
    if final:
        y = _rms(y, gf_ref[...])
    o_ref[...] = y


def _ffn(x, norm_g, wg, wu, wd, final_g=None, tm=1024):
    n, d = x.shape
    nc, _, tf = wg.shape
    tm = min(tm, n)
    final = final_g is not None
    in_specs = [pl.BlockSpec((tm, d), lambda i: (i, 0)), _const_spec((1, d)),
                _const_spec((nc, d, tf)), _const_spec((nc, d, tf)), _const_spec((nc, tf, d))]
    args = [x, norm_g, wg, wu, wd]
    if final:
        in_specs.append(_const_spec((1, d)))
        args.append(final_g)
    return pl.pallas_call(
        functools.partial(_ffn_kernel, final=final),
        grid=(n // tm,),
        in_specs=in_specs,
        out_specs=pl.BlockSpec((tm, d), lambda i: (i, 0)),
        out_shape=jax.ShapeDtypeStruct((n, d), F32),
        scratch_shapes=[pltpu.VMEM((tm, d), BF16), pltpu.VMEM((tm, d), F32)],
        compiler_params=_params(("parallel",)),
        name="ffn_final" if final else "ffn",
    )(*args)


def _mixproj_kernel(x_ref, g_ref, w_ref, gb_ref, cos_ref, sin_ref,
                    z_ref, q_ref, k_ref, v_ref, kb_ref, vb_ref, gate_ref, *, result_layout):
    hb = _rms(x_ref[...], g_ref[...]).astype(BF16)
    for c in range(0, RW_COLS, 256):
        z_ref[:, c:c + 256] = _dot(hb, w_ref[:, c:c + 256])

    cos = cos_ref[...]
    sin = sin_ref[...]
    lane = lax.broadcasted_iota(jnp.int32, cos.shape, 1)
    first_half = (lane & (DA_QK // 2)) == 0

    def rope(xx):
        swapped = jnp.where(first_half, pltpu.roll(xx, LANES - DA_QK // 2, 1), pltpu.roll(xx, DA_QK // 2, 1))
        return xx * cos + swapped * sin

    q_off = RW_COLS
    k_off = q_off + DA_QK_WIDTH
    v_off = k_off + DA_QK_WIDTH
    g_off = v_off + DA_V_WIDTH
    q = _dot(hb, w_ref[:, q_off:q_off + DA_QK_WIDTH])
    k = _dot(hb, w_ref[:, k_off:k_off + DA_QK_WIDTH])
    for c in range(0, DA_QK_WIDTH, LANES):
        q_ref[:, c:c + LANES] = (rope(q[:, c:c + LANES]) * Q_SCALE).astype(BF16)
        kr = rope(k[:, c:c + LANES])
        kb_ref[:, c:c + LANES] = kr.astype(BF16)
        if result_layout:
            kt = kr.T
            head = 2 * (c // LANES)
            k_ref[head] = kt[:DA_QK]
            k_ref[head + 1] = kt[DA_QK:]
        else:
            k_ref[:, c:c + LANES] = kr
    v = _dot(hb, w_ref[:, v_off:v_off + DA_V_WIDTH])
    vb_ref[...] = v.astype(BF16)
    if result_layout:
        for hv in range(DA_HEADS):
            v_ref[pl.ds(hv, v.shape[0], stride=DA_HEADS), :] = v[:, hv * DA_V:(hv + 1) * DA_V]
    else:
        v_ref[...] = v
    for c in range(0, gate_ref.shape[1], 512):
        gate_ref[:, c:c + 512] = jax.nn.sigmoid(
            _dot(hb, w_ref[:, g_off + c:g_off + c + 512]) + gb_ref[:, c:c + 512])


def _mixproj(x, norm_g, w_in, gate_bias, cos_t, sin_t, bsz, t, tm=256):
    n, d = x.shape
    in_cols = w_in.shape[1]
    gate_cols = gate_bias.shape[1]
    tm = min(tm, n)
    result_layout = t % tm == 0 and tm % LANES == 0
    rope_blocks = max(t // tm, 1)
    row = lambda i: (i, 0)
    rope_map = lambda i: (i % rope_blocks, 0)
    outs = [(RW_COLS, F32), (DA_QK_WIDTH, BF16), (DA_QK_WIDTH, F32), (DA_V_WIDTH, F32),
            (DA_QK_WIDTH, BF16), (DA_V_WIDTH, BF16), (gate_cols, F32)]
    out_specs = [pl.BlockSpec((tm, w), row) for w, _ in outs]
    out_shape = [jax.ShapeDtypeStruct((n, w), dt) for w, dt in outs]
    if result_layout:
        out_specs[2] = pl.BlockSpec((None, 2 * DA_HEADS, DA_QK, tm),
                                    lambda i: (i // rope_blocks, 0, 0, i % rope_blocks))
        out_shape[2] = jax.ShapeDtypeStruct((bsz, 2 * DA_HEADS, DA_QK, t), F32)
        out_specs[3] = pl.BlockSpec((tm * DA_HEADS, DA_V), row)
        out_shape[3] = jax.ShapeDtypeStruct((n * DA_HEADS, DA_V), F32)
    res = list(pl.pallas_call(
        functools.partial(_mixproj_kernel, result_layout=result_layout),
        grid=(n // tm,),
        in_specs=[pl.BlockSpec((tm, d), row), _const_spec((1, d)), _const_spec((d, in_cols)),
                  _const_spec((1, gate_cols)),
                  pl.BlockSpec((tm, LANES), rope_map), pl.BlockSpec((tm, LANES), rope_map)],
        out_specs=out_specs,
        out_shape=out_shape,
        compiler_params=_params(("parallel",)),
        name="mixproj",
    )(x, norm_g, w_in, gate_bias, cos_t, sin_t))
    if result_layout:
        res[2] = jnp.transpose(res[2], (0, 3, 1, 2))
    else:
        res[2] = res[2].reshape(bsz, t, 2 * DA_HEADS, DA_QK)
    res[3] = res[3].reshape(bsz, t, DA_HEADS, DA_V)
    return res


def _rw_vectors(z, zp, mu_ref, w0_ref, a0_ref, kk_ref, ka_ref, rk_ref, w2_ref, a2_ref, g2_ref, ones_ref):
    zm = z + (zp - z) * mu_ref[...]
    o1 = RW_WIDTH
    r = zm[:, :o1]
    k = zm[:, o1:2 * o1]
    v = zm[:, 2 * o1:3 * o1]
    lora_in = zm[:, 3 * o1:3 * o1 + LANES]
    gd = zm[:, 3 * o1 + LANES:]
    ones = ones_ref[...]

    x = w0_ref[...] + _dot(jnp.tanh(lora_in).astype(BF16), w2_ref[...])
    neg = -x
    softplus = jnp.maximum(neg, 0.0) + jnp.log1p(jnp.exp(-jnp.abs(neg)))
    dec = jnp.exp(-jnp.exp(-softplus - 0.5))
    a = jax.nn.sigmoid(a0_ref[...] + _dot(lora_in.astype(BF16), a2_ref[...]))
    g = _dot(jax.nn.sigmoid(gd).astype(BF16), g2_ref[...])

    kk = k * kk_ref[...]
    norm = jnp.sqrt(_seg_sum(kk * kk, ones))
    kk = kk / jnp.maximum(norm, 1e-12)
    kmod = k * (1.0 + (a - 1.0) * ka_ref[...])
    bonus = _seg_sum(r * kmod * rk_ref[...], ones) * v
    return r, dec, kmod, v, -kk, kk * a, g, bonus


def _rwprep_kernel(z_ref, zp_ref, *refs):
    outs = _rw_vectors(z_ref[...], zp_ref[...], *refs[:10])
    for o_ref, val in zip(refs[10:], outs):
        o_ref[...] = val


def _rwprep_lanes_kernel(z_ref, zlast_ref, *refs, tt):
    params = refs[:10]
    lane_refs = refs[10:16]
    g_ref, bonus_ref, carry_scr, th_scr = refs[16:]

    @pl.when(pl.program_id(0) == 0)
    def _():
        carry_scr[...] = zlast_ref[...]

    first_row = lax.broadcasted_iota(jnp.int32, (tt, RW_COLS), 0) == 0
    rows = SCAN_ILO * RW_HEADS
    zs, zps = [], []
    for b in range(SCAN_LANE_BATCH):
        z = z_ref[b]
        zs.append(z)
        zps.append(jnp.where(first_row, carry_scr[pl.ds(b, 1), :], pltpu.roll(z, 1, 0)))
        carry_scr[pl.ds(b, 1), :] = z[tt - 1:tt, :]
    r, dec, kmod, v, an, bn, g, bonus = _rw_vectors(jnp.concatenate(zs, axis=0), jnp.concatenate(zps, axis=0), *params)
    for b in range(SCAN_LANE_BATCH):
        seq_rows = slice(b * tt, (b + 1) * tt)
        g_ref[b] = g[seq_rows]
        bonus_ref[b] = bonus[seq_rows]
        for idx, val in enumerate((an, dec, bn, kmod, r, v)):
            for h in range(RW_HEADS):
                th_scr[idx, b, pl.ds(h, tt, stride=RW_HEADS), :] = val[seq_rows, h * RW_HEAD:(h + 1) * RW_HEAD]
    for idx, o_ref in enumerate(lane_refs):
        for tq in range(tt // SCAN_ILO):
            tile = jnp.concatenate([th_scr[idx, b, rows * tq:rows * (tq + 1), :] for b in range(SCAN_LANE_BATCH)],
                                   axis=0)
            o_ref[tq] = tile.T


def _rwprep_lanes(z3, z_last, p, tt=64):
    bsz, t, _ = z3.shape
    tt = min(tt, t)
    vec = _const_spec((1, RW_WIDTH))
    lanes_spec = pl.BlockSpec((tt // SCAN_ILO, RW_HEAD, LANES), lambda i: (i, 0, 0))
    rows_spec = pl.BlockSpec((bsz, tt, RW_WIDTH), lambda i: (0, i, 0))
    outs = pl.pallas_call(
        functools.partial(_rwprep_lanes_kernel, tt=tt),
        grid=(t // tt,),
        in_specs=[pl.BlockSpec((bsz, tt, RW_COLS), lambda i: (0, i, 0)), _const_spec((bsz, RW_COLS)),
                  _const_spec((1, RW_COLS)), vec, vec, vec, vec, vec,
                  _const_spec((LANES, RW_WIDTH)), _const_spec((LANES, RW_WIDTH)),
                  _const_spec((GATE_LORA, RW_WIDTH)), _const_spec((RW_WIDTH, RW_WIDTH))],
        out_specs=[lanes_spec] * 6 + [rows_spec] * 2,
        out_shape=[jax.ShapeDtypeStruct((t // SCAN_ILO, RW_HEAD, LANES), F32)] * 6
        + [jax.ShapeDtypeStruct((bsz, t, RW_WIDTH), F32)] * 2,
        scratch_shapes=[pltpu.VMEM((bsz, RW_COLS), F32),
                        pltpu.VMEM((6, bsz, tt * RW_HEADS, RW_HEAD), F32)],
        compiler_params=_params(("arbitrary",)),
        name="rwprep_lanes",
    )(z3, z_last, p["mu"], p["w0"], p["a0"], p["k_k"], p["k_a"], p["r_k"], p["w2"], p["a2"], p["g2"], p["ones"])
    return outs[:6], outs[6].reshape(bsz * t, RW_WIDTH), outs[7].reshape(bsz * t, RW_WIDTH)


def _rwprep(z, zprev, p, tm=256):
    n = z.shape[0]
    tm = min(tm, n)
    row = lambda i: (i, 0)
    vec = _const_spec((1, RW_WIDTH))
    return pl.pallas_call(
        _rwprep_kernel,
        grid=(n // tm,),
        in_specs=[pl.BlockSpec((tm, RW_COLS), row), pl.BlockSpec((tm, RW_COLS), row),
                  _const_spec((1, RW_COLS)), vec, vec, vec, vec, vec,
                  _const_spec((LANES, RW_WIDTH)), _const_spec((LANES, RW_WIDTH)),
                  _const_spec((GATE_LORA, RW_WIDTH)), _const_spec((RW_WIDTH, RW_WIDTH))],
        out_specs=[pl.BlockSpec((tm, RW_WIDTH), row)] * 8,
        out_shape=[jax.ShapeDtypeStruct((n, RW_WIDTH), F32)] * 8,
        compiler_params=_params(("parallel",)),
        name="rwprep",
    )(z, zprev, p["mu"], p["w0"], p["a0"], p["k_k"], p["k_a"], p["r_k"],
      p["w2"], p["a2"], p["g2"], p["ones"])


def _scan_kernel(a_ref, w_ref, b_ref, k_ref, r_ref, v_ref, s0_ref, spread_ref, y_ref, sf_ref,
                 s_scr, key_scr, val_scr, dots_scr, row_scr, *, tpack):
    tblk = pl.program_id(1)
    tb = y_ref.shape[1]

    @pl.when(tblk == 0)
    def _():
        s_scr[...] = s0_ref[0]

    group_id = lax.broadcasted_iota(jnp.int32, (SCAN_IHI, LANES), 1) // (LANES // SCAN_ILO)

    def prepare(tq, carry):
        a, w, b, k, r, v = (ref[0, tq] for ref in (a_ref, w_ref, b_ref, k_ref, r_ref, v_ref))
        dots = jnp.concatenate([jnp.sum(b * r, axis=0, keepdims=True), jnp.sum(k * r, axis=0, keepdims=True),
                                jnp.zeros((6, LANES), F32)], axis=0)
        x = jnp.concatenate([a, w, b, k, w * r, v, dots], axis=0)
        if tpack > 1:
            hi = x.astype(BF16)
            lo = (x - hi.astype(F32)).astype(BF16)
            x = _dot(jnp.concatenate([hi, lo], axis=1), spread_ref[...])
        for tl in range(tpack):
            t = tq * tpack + tl
            xt = x[:, tl * LANES:(tl + 1) * LANES]
            for qi in range(5):
                key_scr[t, qi] = xt[qi * RW_HEAD:(qi + 1) * RW_HEAD]
            vt = xt[5 * RW_HEAD:6 * RW_HEAD]
            vsel = vt[:SCAN_IHI]
            for g in range(1, SCAN_ILO):
                vsel = jnp.where(group_id == g, vt[g * SCAN_IHI:(g + 1) * SCAN_IHI], vsel)
            val_scr[t] = vsel
            dots_scr[t] = xt[6 * RW_HEAD:6 * RW_HEAD + 2]
        return carry

    n_prep = tb // tpack

    def step(t):
        def group(ih, c):
            s = s_scr[ih]
            sa = jnp.sum(s * key_scr[t, 0], axis=0, keepdims=True)
            row_scr[1, pl.ds(ih, 1), :] = jnp.sum(s * key_scr[t, 4], axis=0, keepdims=True)
            row_scr[0, pl.ds(ih, 1), :] = sa
            s_scr[ih] = s * key_scr[t, 1] + sa * key_scr[t, 2] + val_scr[t, pl.ds(ih, 1), :] * key_scr[t, 3]
            return c

        lax.fori_loop(0, SCAN_IHI, group, 0, unroll=GROUP_UNROLL)
        dots = dots_scr[t]
        y_ref[0, t] = row_scr[1] + row_scr[0] * dots[0:1] + val_scr[t] * dots[1:2]

    def body(t, carry):
        step(t)
        return carry

    lax.fori_loop(0, n_prep, prepare, 0, unroll=2 if n_prep % 2 == 0 else 1)
    lax.fori_loop(0, tb, body, 0)

    @pl.when(tblk == pl.num_programs(1) - 1)
    def _():
        sf_ref[0] = s_scr[...]


def _wkv_scan(a, w, b, k, r, v, s0, t, tpack, tb=64):
    n_p = a.shape[0]
    tb = min(tb, t)
    keyspec = pl.BlockSpec((1, tb // tpack, RW_HEAD, LANES), lambda p, i: (p, i, 0, 0))
    valspec = pl.BlockSpec((1, tb, SCAN_IHI, LANES), lambda p, i: (p, i, 0, 0))
    stspec = pl.BlockSpec((1, SCAN_IHI, RW_HEAD, LANES), lambda p, i: (p, 0, 0, 0))
    out_lane = jnp.arange(SCAN_ILO * LANES)
    tok = out_lane // LANES
    seq = (out_lane % (LANES // SCAN_ILO)) // RW_HEADS
    src_lane = seq * (SCAN_ILO * RW_HEADS) + tok * RW_HEADS + out_lane % RW_HEADS
    spread = jnp.tile((jnp.arange(LANES)[:, None] == src_lane[None, :]).astype(BF16), (2, 1))
    return pl.pallas_call(
        functools.partial(_scan_kernel, tpack=tpack),
        grid=(n_p, t // tb),
        in_specs=[keyspec] * 6 + [stspec, _const_spec(spread.shape)],
        out_specs=[valspec, stspec],
        out_shape=[jax.ShapeDtypeStruct((n_p, t, SCAN_IHI, LANES), F32),
                   jax.ShapeDtypeStruct((n_p, SCAN_IHI, RW_HEAD, LANES), F32)],
        scratch_shapes=[pltpu.VMEM((SCAN_IHI, RW_HEAD, LANES), F32), pltpu.VMEM((tb, 5, RW_HEAD, LANES), F32),
                        pltpu.VMEM((tb, SCAN_IHI, LANES), F32), pltpu.VMEM((tb, 2, LANES), F32),
                        pltpu.VMEM((2, SCAN_IHI, LANES), F32)],
        compiler_params=_params(("parallel", "arbitrary")),
        name="wkv_scan",
    )(a, w, b, k, r, v, s0, spread)


def _to_copied_lanes(x, bsz, t):
    g = bsz // SCAN_LANE_BATCH
    x = x.reshape(g, SCAN_LANE_BATCH, t, RW_HEADS, RW_HEAD).transpose(0, 2, 4, 1, 3)
    x = jnp.broadcast_to(x[:, :, :, None], (g, t, RW_HEAD, SCAN_ILO, SCAN_LANE_BATCH, RW_HEADS))
    return x.reshape(g, t, RW_HEAD, LANES)


def _from_value_lanes(y, bsz, t):
    g = bsz // SCAN_LANE_BATCH
    y = y.reshape(g, t, SCAN_IHI, SCAN_ILO, SCAN_LANE_BATCH, RW_HEADS).transpose(0, 4, 1, 5, 3, 2)
    return y.reshape(bsz * t, RW_WIDTH)


def _state_to_lanes(s):
    bsz = s.shape[0]
    g = bsz // SCAN_LANE_BATCH
    s = s.reshape(g, SCAN_LANE_BATCH, RW_HEADS, SCAN_ILO, SCAN_IHI, RW_HEAD).transpose(0, 4, 5, 3, 1, 2)
    return s.reshape(g, SCAN_IHI, RW_HEAD, LANES)


def _state_from_lanes(s, bsz):
    g = bsz // SCAN_LANE_BATCH
    s = s.reshape(g, SCAN_IHI, RW_HEAD, SCAN_ILO, SCAN_LANE_BATCH, RW_HEADS).transpose(0, 4, 5, 3, 1, 2)
    return s.reshape(bsz, RW_HEADS, RW_HEAD, RW_HEAD)


def _merge_kernel(x_ref, ys_ref, bonus_ref, g_ref, yb_ref, ga_ref, gb_ref,
                  lnw_ref, lnb_ref, ones_ref, wpa_ref, wpb_ref, wout_ref, o_ref):
    ones = ones_ref[...]
    y = ys_ref[...]
    inv_n = 1.0 / RW_HEAD
    mu = _seg_sum(y, ones) * inv_n
    d = y - mu
    var = _seg_sum(d * d, ones) * inv_n
    ya = d * lax.rsqrt(var + RW_GN_EPS) * lnw_ref[...] + lnb_ref[...]
    ya = (ya + bonus_ref[...]) * g_ref[...]
    merged = (ga_ref[...] * _dot(ya.astype(BF16), wpa_ref[...])
              + gb_ref[...] * _dot(yb_ref[...].astype(BF16), wpb_ref[...]))
    o_ref[...] = x_ref[...] + _dot(merged.astype(BF16), wout_ref[...])


def _merge(x, ys, bonus, g, yb, gates, p, tm=256):
    n, d = x.shape
    tm = min(tm, n)
    row = lambda i: (i, 0)
    half = pl.BlockSpec((tm, RW_WIDTH), row)
    return pl.pallas_call(
        _merge_kernel,
        grid=(n // tm,),
        in_specs=[pl.BlockSpec((tm, d), row), half, half, half, half,
                  pl.BlockSpec((tm, d), lambda i: (i, 0)), pl.BlockSpec((tm, d), lambda i: (i, 1)),
                  _const_spec((1, RW_WIDTH)), _const_spec((1, RW_WIDTH)), _const_spec((RW_WIDTH, RW_WIDTH)),
                  _const_spec((RW_WIDTH, d)), _const_spec((DA_V_WIDTH, d)), _const_spec((d, d))],
        out_specs=pl.BlockSpec((tm, d), row),
        out_shape=jax.ShapeDtypeStruct((n, d), F32),
        compiler_params=_params(("parallel",)),
        name="merge",
    )(x, ys, bonus, g, yb, gates, gates, p["ln_w"], p["ln_b"], p["ones"], p["wpa"], p["wpb"], p["wout"])


def _flash_kernel(lam_ref, q_ref, k_ref, v_ref, sub_ref, o_ref, m_scr, l_scr, acc_scr, *, tq, out_scale):
    qi = pl.program_id(2)
    q = q_ref[...]
    lane = lax.broadcasted_iota(jnp.int32, q.shape, 1)
    zero = jnp.zeros_like(q)
    q2 = jnp.concatenate([jnp.where(lane < DA_QK, q, zero), jnp.where(lane >= DA_QK, q, zero)], axis=0)
    m_scr[...] = jnp.full_like(m_scr, -jnp.inf)
    l_scr[...] = jnp.zeros_like(l_scr)
    acc_scr[...] = jnp.zeros_like(acc_scr)
    nt = (((1,), (1,)), ((), ()))
    reps = tq // LANES

    def block(j, diagonal):
        off = pl.multiple_of(j * tq, tq)
        kb = k_ref[pl.ds(off, tq), :]
        vb = v_ref[pl.ds(off, tq), :]
        s = lax.dot_general(q2, kb, nt, preferred_element_type=F32)
        if diagonal:
            rr = lax.broadcasted_iota(jnp.int32, s.shape, 0) & (tq - 1)
            cc = lax.broadcasted_iota(jnp.int32, s.shape, 1)
            s = jnp.where(cc <= rr, s, -jnp.inf)
        m_old = m_scr[...]
        m_new = jnp.maximum(m_old, jnp.max(s, axis=-1, keepdims=True))
        alpha = jnp.exp2(m_old - m_new)
        p = jnp.exp2(s - jnp.concatenate([m_new] * reps, axis=1))
        l_scr[...] = alpha * l_scr[...] + jnp.sum(p, axis=-1, keepdims=True)
        acc_scr[...] = alpha * acc_scr[...] + _dot(p.astype(BF16), vb)
        m_scr[...] = m_new

    def body(j, carry):
        block(j, False)
        return carry

    lax.fori_loop(0, qi, body, 0)
    block(qi, True)

    o = acc_scr[...] / l_scr[...]
    o = o[:tq] - lam_ref[...] * o[tq:]
    o_ref[...] = _rms(o, sub_ref[...]) * out_scale


def _flash(lam_row, qb, kb, vb, subln, bsz, s, out_scale, tq=512):
    tq = min(tq, s)
    q3 = qb.reshape(bsz, s, DA_QK_WIDTH)
    k3 = kb.reshape(bsz, s, DA_QK_WIDTH)
    v3 = vb.reshape(bsz, s, DA_V_WIDTH)
    out = pl.pallas_call(
        functools.partial(_flash_kernel, tq=tq, out_scale=out_scale),
        grid=(bsz, DA_HEADS, s // tq),
        in_specs=[_const_spec((1, LANES)),
                  pl.BlockSpec((None, tq, LANES), lambda b, h, i: (b, i, h)),
                  pl.BlockSpec((None, s, LANES), lambda b, h, i: (b, 0, h)),
                  pl.BlockSpec((None, s, DA_V), lambda b, h, i: (b, 0, h)),
                  pl.BlockSpec((1, DA_V), lambda b, h, i: (0, h))],
        out_specs=pl.BlockSpec((None, tq, DA_V), lambda b, h, i: (b, i, h)),
        out_shape=jax.ShapeDtypeStruct((bsz, s, DA_V_WIDTH), F32),
        scratch_shapes=[pltpu.VMEM((2 * tq, LANES), F32), pltpu.VMEM((2 * tq, LANES), F32),
                        pltpu.VMEM((2 * tq, DA_V), F32)],
        compiler_params=_params(("parallel", "parallel", "arbitrary")),
        name="flash_diff",
    )(lam_row, q3, k3, v3, subln)
    return out.reshape(bsz * s, DA_V_WIDTH)


def _decode_kernel(pt_ref, lam_ref, q_ref, qd_ref, kn_ref, vn_ref, sub_ref, *rest, pages, out_scale):
    k_refs = rest[:pages]
    v_refs = rest[pages:2 * pages]
    o_ref, m_scr, l_scr, acc_scr = rest[2 * pages:]
    j = pl.program_id(1)
    own_head = lax.broadcasted_iota(jnp.int32, acc_scr.shape, 0) >> 1

    @pl.when(j == 0)
    def _():
        m_scr[...] = jnp.sum(q_ref[...] * kn_ref[...], axis=-1, keepdims=True)
        l_scr[...] = jnp.ones_like(l_scr)
        acc_scr[...] = vn_ref[...]

    qd = qd_ref[...]
    s = jnp.concatenate([_dot(qd, k_refs[u][...].astype(BF16)) for u in range(pages)], axis=1)
    m_old = m_scr[...]
    m_new = jnp.maximum(m_old, jnp.max(s, axis=-1, keepdims=True))
    alpha = jnp.exp2(m_old - m_new)
    p = jnp.exp2(s - m_new)
    l_scr[...] = alpha * l_scr[...] + jnp.sum(p, axis=-1, keepdims=True)
    m_scr[...] = m_new
    pv = jnp.zeros(acc_scr.shape, F32)
    for u in range(pages):
        pu = p[:, u * PAGE_SIZE:(u + 1) * PAGE_SIZE].astype(BF16)
        for hv in range(DA_HEADS):
            vh = v_refs[u][pl.ds(hv, PAGE_SIZE, stride=DA_HEADS), :].astype(BF16)
            pv = pv + jnp.where(own_head == hv, _dot(pu, vh), 0.0)
    acc_scr[...] = alpha * acc_scr[...] + pv

    @pl.when(j == pl.num_programs(1) - 1)
    def _():
        o8 = acc_scr[...] / l_scr[...]
        o = o8 - lam_ref[...] * pltpu.roll(o8, 2 * DA_HEADS - 1, 0)
        o_ref[...] = _rms(o, sub_ref[...]) * out_scale


def _decode_attn(lam_row, q8, k_new8, v_new8, subln8, cache_kt, cache_v, page_table, out_scale, pages=16):
    bsz, n_pages = page_table.shape
    pages = min(pages, n_pages)
    pt = page_table.reshape(-1)
    n_heads = 2 * DA_HEADS

    def page_map(u):
        return lambda b, j, pt_ref: (pt_ref[b * n_pages + j * pages + u], 0, 0)

    q_diag = (q8[:, :, None, :] * jnp.eye(n_heads, dtype=F32)[None, :, :, None]).reshape(bsz, n_heads, -1)
    seq = lambda b, j, pt_ref: (b, 0, 0)
    fixed = lambda b, j, pt_ref: (0, 0)
    in_specs = [pl.BlockSpec((1, LANES), fixed),
                pl.BlockSpec((None, n_heads, DA_QK), seq),
                pl.BlockSpec((None, n_heads, n_heads * DA_QK), seq),
                pl.BlockSpec((None, n_heads, DA_QK), seq),
                pl.BlockSpec((None, n_heads, DA_V), seq),
                pl.BlockSpec((n_heads, DA_V), fixed)]
    in_specs += [pl.BlockSpec((None, n_heads * DA_QK, PAGE_SIZE), page_map(u)) for u in range(pages)]
    in_specs += [pl.BlockSpec((None, PAGE_SIZE * DA_HEADS, DA_V), page_map(u)) for u in range(pages)]
    grid_spec = pltpu.PrefetchScalarGridSpec(
        num_scalar_prefetch=1,
        grid=(bsz, n_pages // pages),
        in_specs=in_specs,
        out_specs=pl.BlockSpec((None, n_heads, DA_V), seq),
        scratch_shapes=[pltpu.VMEM((n_heads, 1), F32), pltpu.VMEM((n_heads, 1), F32),
                        pltpu.VMEM((n_heads, DA_V), F32)])
    return pl.pallas_call(
        functools.partial(_decode_kernel, pages=pages, out_scale=out_scale),
        grid_spec=grid_spec,
        out_shape=jax.ShapeDtypeStruct((bsz, n_heads, DA_V), F32),
        compiler_params=_params(("parallel", "arbitrary")),
        name="decode_attn",
    )(pt, lam_row, q8, q_diag.astype(BF16), k_new8, v_new8, subln8,
      *([cache_kt] * pages), *([cache_v] * pages))


def _rope_tables(pos):
    half = DA_QK // 2
    inv = ROPE_THETA ** (-jnp.arange(half, dtype=F32) / half)
    ang = pos.astype(F32)[:, None] * inv[None, :]
    cos, sin = jnp.cos(ang), jnp.sin(ang)
    reps = LANES // DA_QK
    return jnp.tile(jnp.concatenate([cos, cos], -1), (1, reps)), jnp.tile(jnp.concatenate([-sin, sin], -1), (1, reps))


def _layer_params(l, ffn1_norm, ffn1_up, ffn1_down, mix_norm, w_in, gate_bias, rw_mu, rw_w0, rw_w2, rw_a0,
                  rw_a2, rw_g2, rw_k_k, rw_k_a, rw_r_k, rw_ln_w, rw_ln_b, da_subln, w_proj_a, w_proj_b, w_out,
                  ffn2_norm, ffn2_up, ffn2_down):
    def ffn_w(up, down):
        d, two_ff = up.shape
        ff = two_ff // 2
        tf = 256
        nc = ff // tf
        split = lambda w: w.reshape(d, nc, tf).transpose(1, 0, 2).astype(BF16)
        return split(up[:, :ff]), split(up[:, ff:]), down.reshape(nc, tf, d).astype(BF16)

    row = lambda v: v.reshape(1, -1)
    head_id = jnp.arange(RW_WIDTH) // RW_HEAD
    zeros = jnp.zeros((DECAY_LORA, RW_WIDTH), F32)
    return dict(
        ffn1=(row(ffn1_norm[l]),) + ffn_w(ffn1_up[l], ffn1_down[l]),
        ffn2=(row(ffn2_norm[l]),) + ffn_w(ffn2_up[l], ffn2_down[l]),
        mix_norm=row(mix_norm[l]), w_in=w_in[l].astype(BF16), gate_bias=row(gate_bias[l]),
        mu=row(rw_mu[l]), w0=row(rw_w0[l]), a0=row(rw_a0[l]), k_k=row(rw_k_k[l]), k_a=row(rw_k_a[l]),
        r_k=row(rw_r_k[l]),
        w2=jnp.concatenate([rw_w2[l], zeros], 0).astype(BF16),
        a2=jnp.concatenate([zeros, rw_a2[l]], 0).astype(BF16),
        g2=rw_g2[l].astype(BF16),
        ones=(head_id[:, None] == head_id[None, :]).astype(BF16),
        ln_w=row(rw_ln_w[l]), ln_b=row(rw_ln_b[l]), subln=row(da_subln[l]),
        wpa=w_proj_a[l].astype(BF16), wpb=w_proj_b[l].astype(BF16), wout=w_out[l].astype(BF16))


def _trunk_front(x, p, cos_t, sin_t, z_last, bsz, t):
    x1 = _ffn(x, *p["ffn1"])
    z, qb, k, v, kb, vb, gates = _mixproj(x1, p["mix_norm"], p["w_in"], p["gate_bias"], cos_t, sin_t, bsz, t)
    z3 = z.reshape(bsz, t, RW_COLS)
    if bsz == SCAN_LANE_BATCH and t % SCAN_ILO == 0:
        tpack = SCAN_ILO
        ops, g, bonus = _rwprep_lanes(z3, z_last, p)
        ops = [u[None] for u in ops]
    else:
        assert t == 1
        tpack = 1
        r, dec, kmod, vv, an, bn, g, bonus = _rwprep(z, z_last, p)
        ops = [_to_copied_lanes(u, bsz, t) for u in (an, dec, bn, kmod, r, vv)]
    return x1, z3, qb, k, v, kb, vb, gates, (ops, tpack), g, bonus


def _rwkv_scan(rw, wkv0, bsz, t):
    ops, tpack = rw
    y, s_fin = _wkv_scan(*ops, _state_to_lanes(wkv0), t, tpack)
    return _from_value_lanes(y, bsz, t), _state_from_lanes(s_fin, bsz)


def kernel(x_prompt, x_sample, cache_k, cache_v, state_wkv, state_shift, page_table, ffn1_norm, ffn1_up, ffn1_down, mix_norm, w_in, gate_bias, rw_mu, rw_w0, rw_w2, rw_a0, rw_a2, rw_g2, rw_k_k, rw_k_a, rw_r_k, rw_ln_w, rw_ln_b, da_lq1, da_lk1, da_lq2, da_lk2, da_subln, w_proj_a, w_proj_b, w_out, ffn2_norm, ffn2_up, ffn2_down, final_norm):
    bp, sp, d = x_prompt.shape
    bs, ts, _ = x_sample.shape
    depth = ffn1_norm.shape[0]
    assert ts == 1 and bp % SCAN_LANE_BATCH == 0 and bs % SCAN_LANE_BATCH == 0
    assert cache_k.shape[2] == PAGE_SIZE == LANES
    n_past = page_table.shape[1] * PAGE_SIZE
    cos_p, sin_p = _rope_tables(jnp.arange(sp, dtype=jnp.int32))
    cos_s, sin_s = _rope_tables(jnp.full((bs,), n_past, dtype=jnp.int32))
    final_row = final_norm.reshape(1, d)

    hp = x_prompt.reshape(bp * sp, d)
    hs = x_sample.reshape(bs * ts, d)
    outs = [[] for _ in range(8)]
    for l in range(depth):
        p = _layer_params(l, ffn1_norm, ffn1_up, ffn1_down, mix_norm, w_in, gate_bias, rw_mu, rw_w0, rw_w2,
                          rw_a0, rw_a2, rw_g2, rw_k_k, rw_k_a, rw_r_k, rw_ln_w, rw_ln_b, da_subln, w_proj_a,
                          w_proj_b, w_out, ffn2_norm, ffn2_up, ffn2_down)
        lam_init = 0.8 - 0.6 * math.exp(-0.3 * l)
        lam = (jnp.exp(jnp.sum(da_lq1[l] * da_lk1[l])) - jnp.exp(jnp.sum(da_lq2[l] * da_lk2[l])) + lam_init)
        lam_row = jnp.full((1, LANES), lam, F32)
        out_scale = 1.0 - lam_init
        last = l == depth - 1

        x1, z3, qb, k, v, kb, vb, gates, rw, g, bonus = _trunk_front(
            hp, p, cos_p, sin_p, jnp.zeros((bp, RW_COLS), F32), bp, sp)
        ys, wkv_p = _rwkv_scan(rw, jnp.zeros((bp, RW_HEADS, RW_HEAD, RW_HEAD), F32), bp, sp)
        yb = _flash(lam_row, qb, kb, vb, p["subln"], bp, sp, out_scale)
        x2 = _merge(x1, ys, bonus, g, yb, gates, p)
        hp = _ffn(x2, *p["ffn2"], final_g=final_row if last else None)
        outs[0].append(k)
        outs[1].append(v)
        outs[2].append(wkv_p)
        outs[3].append(z3[:, -1])

        x1, z3, qb, k, v, kb, vb, gates, rw, g, bonus = _trunk_front(
            hs, p, cos_s, sin_s, state_shift[l], bs, ts)
        ys, wkv_s = _rwkv_scan(rw, state_wkv[l], bs, ts)
        q8 = qb.astype(F32).reshape(bs, 2 * DA_HEADS, DA_QK)
        k8 = k.reshape(bs, 2 * DA_HEADS, DA_QK)
        v8 = jnp.repeat(v.reshape(bs, DA_HEADS, DA_V), 2, axis=1)
        sub8 = jnp.repeat(p["subln"].reshape(DA_HEADS, DA_V), 2, axis=0)
        cache_vf = cache_v[l].reshape(cache_v.shape[1], PAGE_SIZE * DA_HEADS, DA_V)
        cache_kt = jnp.transpose(cache_k[l], (0, 2, 3, 1)).reshape(cache_k.shape[1], DA_QK_WIDTH, PAGE_SIZE)
        o8 = _decode_attn(lam_row, q8, k8, v8, sub8, cache_kt, cache_vf, page_table, out_scale)
        yb = o8[:, ::2].reshape(bs, DA_V_WIDTH)
        x2 = _merge(x1, ys, bonus, g, yb, gates, p)
        hs = _ffn(x2, *p["ffn2"], final_g=final_row if last else None)
        outs[4].append(k)
        outs[5].append(v)
        outs[6].append(wkv_s)
        outs[7].append(z3[:, -1])

    st = [jnp.stack(o) for o in outs]
    return (hp.reshape(bp, sp, d), hs.reshape(bs, ts, d), st[0], st[1], st[2], st[3], st[4], st[5], st[6], st[7])
```

```python
import functools
import math

import jax
import jax.numpy as jnp
from jax import lax
from jax.experimental import pallas as pl
from jax.experimental.pallas import tpu as pltpu

F32 = jnp.float32
BF16 = jnp.bfloat16

NORM_EPS = 1e-6
ROPE_THETA = 10000.0
PAGE_SIZE = 128

RW_HEADS = 8
RW_HEAD = 64
RW_WIDTH = RW_HEADS * RW_HEAD
DECAY_LORA = 64
AAA_LORA = 64
GATE_LORA = 128
RW_COLS = 3 * RW_WIDTH + DECAY_LORA + AAA_LORA + GATE_LORA
RW_GN_EPS = 1e-5 * RW_HEAD * RW_HEAD
DA_HEADS = 4
DA_QK = 64
DA_V = 2 * DA_QK
DA_QK_WIDTH = 2 * DA_HEADS * DA_QK
DA_V_WIDTH = DA_HEADS * DA_V
Q_SCALE = DA_QK ** -0.5 * math.log2(math.e)

LANES = 128
SCAN_LANE_BATCH = 4
SCAN_ILO = LANES // (SCAN_LANE_BATCH * RW_HEADS)
SCAN_IHI = RW_HEAD // SCAN_ILO
GROUP_UNROLL = 16
VMEM_LIMIT = 56 * 1024 * 1024


def _const_spec(shape):
    nd = len(shape)
    return pl.BlockSpec(shape, lambda *_: (0,) * nd, pipeline_mode=pl.Buffered(1))


def _params(sem):
    return pltpu.CompilerParams(dimension_semantics=sem, vmem_limit_bytes=VMEM_LIMIT)


def _rms(x, g):
    return x * lax.rsqrt(jnp.mean(x * x, axis=-1, keepdims=True) + NORM_EPS) * g


def _dot(a, b):
    return jnp.dot(a, b, preferred_element_type=F32)


def _seg_sum(x, ones_bf):
    hi = x.astype(BF16)
    lo = (x - hi.astype(F32)).astype(BF16)
    return _dot(hi, ones_bf) + _dot(lo, ones_bf)


def _ffn_kernel(x_ref, g_ref, wg_ref, wu_ref, wd_ref, *rest, final):
    if final:
        gf_ref, o_ref, h_scr, acc_scr = rest
    else:
        o_ref, h_scr, acc_scr = rest
    x = x_ref[...]
    h_scr[...] = _rms(x, g_ref[...]).astype(BF16)
    acc_scr[...] = jnp.zeros_like(acc_scr)

    def body(c, carry):
        hb = h_scr[...]
        gate = _dot(hb, wg_ref[c])
        up = _dot(hb, wu_ref[c])
        act = (gate * jax.nn.sigmoid(gate) * up).astype(BF16)
        acc_scr[...] += _dot(act, wd_ref[c])
        return carry

    lax.fori_loop(0, wg_ref.shape[0], body, 0)
    y = x_ref[...] + 0.5 * acc_scr[...]
    if final:
        y = _rms(y, gf_ref[...])
    o_ref[...] = y


def _ffn(x, norm_g, wg, wu, wd, final_g=None, tm=1024):
    n, d = x.shape
    nc, _, tf = wg.shape
    tm = min(tm, n)
    final = final_g is not None
    in_specs = [pl.BlockSpec((tm, d), lambda i: (i, 0)), _const_spec((1, d)),
                _const_spec((nc, d, tf)), _const_spec((nc, d, tf)), _const_spec((nc, tf, d))]
    args = [x, norm_g, wg, wu, wd]
    if final:
        in_specs.append(_const_spec((1, d)))
        args.append(final_g)
    return pl.pallas_call(
        functools.partial(_ffn_kernel, final=final),
        grid=(n // tm,),
        in_specs=in_specs,
        out_specs=pl.BlockSpec((tm, d), lambda i: (i, 0)),
        out_shape=jax.ShapeDtypeStruct((n, d), F32),
        scratch_shapes=[pltpu.VMEM((tm, d), BF16), pltpu.VMEM((tm, d), F32)],
        compiler_params=_params(("parallel",)),
        name="ffn_final" if final else "ffn",
    )(*args)


def _mixproj_kernel(x_ref, g_ref, w_ref, gb_ref, cos_ref, sin_ref,
                    z_ref, q_ref, k_ref, v_ref, kb_ref, vb_ref, gate_ref, *, result_layout):
    hb = _rms(x_ref[...], g_ref[...]).astype(BF16)
    for c in range(0, RW_COLS, 256):
        z_ref[:, c:c + 256] = _dot(hb, w_ref[:, c:c + 256])

    cos = cos_ref[...]
    sin = sin_ref[...]
    lane = lax.broadcasted_iota(jnp.int32, cos.shape, 1)
    first_half = (lane & (DA_QK // 2)) == 0

    def rope(xx):
        swapped = jnp.where(first_half, pltpu.roll(xx, LANES - DA_QK // 2, 1), pltpu.roll(xx, DA_QK // 2, 1))
        return xx * cos + swapped * sin

    q_off = RW_COLS
    k_off = q_off + DA_QK_WIDTH
    v_off = k_off + DA_QK_WIDTH
    g_off = v_off + DA_V_WIDTH
    q = _dot(hb, w_ref[:, q_off:q_off + DA_QK_WIDTH])
    k = _dot(hb, w_ref[:, k_off:k_off + DA_QK_WIDTH])
    for c in range(0, DA_QK_WIDTH, LANES):
        q_ref[:, c:c + LANES] = (rope(q[:, c:c + LANES]) * Q_SCALE).astype(BF16)
        kr = rope(k[:, c:c + LANES])
        kb_ref[:, c:c + LANES] = kr.astype(BF16)
        if result_layout:
            kt = kr.T
            head = 2 * (c // LANES)
            k_ref[head] = kt[:DA_QK]
            k_ref[head + 1] = kt[DA_QK:]
        else:
            k_ref[:, c:c + LANES] = kr
    v = _dot(hb, w_ref[:, v_off:v_off + DA_V_WIDTH])
    vb_ref[...] = v.astype(BF16)
    if result_layout:
        for hv in range(DA_HEADS):
            v_ref[pl.ds(hv, v.shape[0], stride=DA_HEADS), :] = v[:, hv * DA_V:(hv + 1) * DA_V]
    else:
        v_ref[...] = v
    for c in range(0, gate_ref.shape[1], 512):
        gate_ref[:, c:c + 512] = jax.nn.sigmoid(
            _dot(hb, w_ref[:, g_off + c:g_off + c + 512]) + gb_ref[:, c:c + 512])


def _mixproj(x, norm_g, w_in, gate_bias, cos_t, sin_t, bsz, t, tm=256):
    n, d = x.shape
    in_cols = w_in.shape[1]
    gate_cols = gate_bias.shape[1]
    tm = min(tm, n)
    result_layout = t % tm == 0 and tm % LANES == 0
    rope_blocks = max(t // tm, 1)
    row = lambda i: (i, 0)
    rope_map = lambda i: (i % rope_blocks, 0)
    outs = [(RW_COLS, F32), (DA_QK_WIDTH, BF16), (DA_QK_WIDTH, F32), (DA_V_WIDTH, F32),
            (DA_QK_WIDTH, BF16), (DA_V_WIDTH, BF16), (gate_cols, F32)]
    out_specs = [pl.BlockSpec((tm, w), row) for w, _ in outs]
    out_shape = [jax.ShapeDtypeStruct((n, w), dt) for w, dt in outs]
    if result_layout:
        out_specs[2] = pl.BlockSpec((None, 2 * DA_HEADS, DA_QK, tm),
                                    lambda i: (i // rope_blocks, 0, 0, i % rope_blocks))
        out_shape[2] = jax.ShapeDtypeStruct((bsz, 2 * DA_HEADS, DA_QK, t), F32)
        out_specs[3] = pl.BlockSpec((tm * DA_HEADS, DA_V), row)
        out_shape[3] = jax.ShapeDtypeStruct((n * DA_HEADS, DA_V), F32)
    res = list(pl.pallas_call(
        functools.partial(_mixproj_kernel, result_layout=result_layout),
        grid=(n // tm,),
        in_specs=[pl.BlockSpec((tm, d), row), _const_spec((1, d)), _const_spec((d, in_cols)),
                  _const_spec((1, gate_cols)),
                  pl.BlockSpec((tm, LANES), rope_map), pl.BlockSpec((tm, LANES), rope_map)],
        out_specs=out_specs,
        out_shape=out_shape,
        compiler_params=_params(("parallel",)),
        name="mixproj",
    )(x, norm_g, w_in, gate_bias, cos_t, sin_t))
    if result_layout:
        res[2] = jnp.transpose(res[2], (0, 3, 1, 2))
    else:
        res[2] = res[2].reshape(bsz, t, 2 * DA_HEADS, DA_QK)
    res[3] = res[3].reshape(bsz, t, DA_HEADS, DA_V)
    return res


def _rw_vectors(z, zp, mu_ref, w0_ref, a0_ref, kk_ref, ka_ref, rk_ref, w2_ref, a2_ref, g2_ref, ones_ref):
    zm = z + (zp - z) * mu_ref[...]
    o1 = RW_WIDTH
    r = zm[:, :o1]
    k = zm[:, o1:2 * o1]
    v = zm[:, 2 * o1:3 * o1]
    lora_in = zm[:, 3 * o1:3 * o1 + LANES]
    gd = zm[:, 3 * o1 + LANES:]
    ones = ones_ref[...]

    x = w0_ref[...] + _dot(jnp.tanh(lora_in).astype(BF16), w2_ref[...])
    neg = -x
    softplus = jnp.maximum(neg, 0.0) + jnp.log1p(jnp.exp(-jnp.abs(neg)))
    dec = jnp.exp(-jnp.exp(-softplus - 0.5))
    a = jax.nn.sigmoid(a0_ref[...] + _dot(lora_in.astype(BF16), a2_ref[...]))
    g = _dot(jax.nn.sigmoid(gd).astype(BF16), g2_ref[...])

    kk = k * kk_ref[...]
    norm = jnp.sqrt(_seg_sum(kk * kk, ones))
    kk = kk / jnp.maximum(norm, 1e-12)
    kmod = k * (1.0 + (a - 1.0) * ka_ref[...])
    bonus = _seg_sum(r * kmod * rk_ref[...], ones) * v
    return r, dec, kmod, v, -kk, kk * a, g, bonus


def _rwprep_kernel(z_ref, zp_ref, *refs):
    outs = _rw_vectors(z_ref[...], zp_ref[...], *refs[:10])
    for o_ref, val in zip(refs[10:], outs):
        o_ref[...] = val


def _rwprep_lanes_kernel(z_ref, zlast_ref, *refs, tt):
    params = refs[:10]
    lane_refs = refs[10:16]
    g_ref, bonus_ref, carry_scr, th_scr = refs[16:]

    @pl.when(pl.program_id(0) == 0)
    def _():
        carry_scr[...] = zlast_ref[...]

    first_row = lax.broadcasted_iota(jnp.int32, (tt, RW_COLS), 0) == 0
    rows = SCAN_ILO * RW_HEADS
    zs, zps = [], []
    for b in range(SCAN_LANE_BATCH):
        z = z_ref[b]
        zs.append(z)
        zps.append(jnp.where(first_row, carry_scr[pl.ds(b, 1), :], pltpu.roll(z, 1, 0)))
        carry_scr[pl.ds(b, 1), :] = z[tt - 1:tt, :]
    r, dec, kmod, v, an, bn, g, bonus = _rw_vectors(jnp.concatenate(zs, axis=0), jnp.concatenate(zps, axis=0), *params)
    for b in range(SCAN_LANE_BATCH):
        seq_rows = slice(b * tt, (b + 1) * tt)
        g_ref[b] = g[seq_rows]
        bonus_ref[b] = bonus[seq_rows]
        for idx, val in enumerate((an, dec, bn, kmod, r, v)):
            for h in range(RW_HEADS):
                th_scr[idx, b, pl.ds(h, tt, stride=RW_HEADS), :] = val[seq_rows, h * RW_HEAD:(h + 1) * RW_HEAD]
    for idx, o_ref in enumerate(lane_refs):
        for tq in range(tt // SCAN_ILO):
            tile = jnp.concatenate([th_scr[idx, b, rows * tq:rows * (tq + 1), :] for b in range(SCAN_LANE_BATCH)],
                                   axis=0)
            o_ref[tq] = tile.T


def _rwprep_lanes(z3, z_last, p, tt=64):
    bsz, t, _ = z3.shape
    tt = min(tt, t)
    vec = _const_spec((1, RW_WIDTH))
    lanes_spec = pl.BlockSpec((tt // SCAN_ILO, RW_HEAD, LANES), lambda i: (i, 0, 0))
    rows_spec = pl.BlockSpec((bsz, tt, RW_WIDTH), lambda i: (0, i, 0))
    outs = pl.pallas_call(
        functools.partial(_rwprep_lanes_kernel, tt=tt),
        grid=(t // tt,),
        in_specs=[pl.BlockSpec((bsz, tt, RW_COLS), lambda i: (0, i, 0)), _const_spec((bsz, RW_COLS)),
                  _const_spec((1, RW_COLS)), vec, vec, vec, vec, vec,
                  _const_spec((LANES, RW_WIDTH)), _const_spec((LANES, RW_WIDTH)),
                  _const_spec((GATE_LORA, RW_WIDTH)), _const_spec((RW_WIDTH, RW_WIDTH))],
        out_specs=[lanes_spec] * 6 + [rows_spec] * 2,
        out_shape=[jax.ShapeDtypeStruct((t // SCAN_ILO, RW_HEAD, LANES), F32)] * 6
        + [jax.ShapeDtypeStruct((bsz, t, RW_WIDTH), F32)] * 2,
        scratch_shapes=[pltpu.VMEM((bsz, RW_COLS), F32),
                        pltpu.VMEM((6, bsz, tt * RW_HEADS, RW_HEAD), F32)],
        compiler_params=_params(("arbitrary",)),
        name="rwprep_lanes",
    )(z3, z_last, p["mu"], p["w0"], p["a0"], p["k_k"], p["k_a"], p["r_k"], p["w2"], p["a2"], p["g2"], p["ones"])
    return outs[:6], outs[6].reshape(bsz * t, RW_WIDTH), outs[7].reshape(bsz * t, RW_WIDTH)


def _rwprep(z, zprev, p, tm=256):
    n = z.shape[0]
    tm = min(tm, n)
    row = lambda i: (i, 0)
    vec = _const_spec((1, RW_WIDTH))
    return pl.pallas_call(
        _rwprep_kernel,
        grid=(n // tm,),
        in_specs=[pl.BlockSpec((tm, RW_COLS), row), pl.BlockSpec((tm, RW_COLS), row),
                  _const_spec((1, RW_COLS)), vec, vec, vec, vec, vec,
                  _const_spec((LANES, RW_WIDTH)), _const_spec((LANES, RW_WIDTH)),
                  _const_spec((GATE_LORA, RW_WIDTH)), _const_spec((RW_WIDTH, RW_WIDTH))],
        out_specs=[pl.BlockSpec((tm, RW_WIDTH), row)] * 8,
        out_shape=[jax.ShapeDtypeStruct((n, RW_WIDTH), F32)] * 8,
        compiler_params=_params(("parallel",)),
        name="rwprep",
    )(z, zprev, p["mu"], p["w0"], p["a0"], p["k_k"], p["k_a"], p["r_k"],
      p["w2"], p["a2"], p["g2"], p["ones"])


def _scan_kernel(a_ref, w_ref, b_ref, k_ref, r_ref, v_ref, s0_ref, spread_ref, y_ref, sf_ref,
                 s_scr, key_scr, val_scr, dots_scr, row_scr, part_scr, *, tpack):
    tblk = pl.program_id(1)
    tb = y_ref.shape[1]

    @pl.when(tblk == 0)
    def _():
        s_scr[...] = s0_ref[0]

    group_id = lax.broadcasted_iota(jnp.int32, (SCAN_IHI, LANES), 1) // (LANES // SCAN_ILO)

    def prepare(tq, carry):
        a, w, b, k, r, v = (ref[0, tq] for ref in (a_ref, w_ref, b_ref, k_ref, r_ref, v_ref))
        dots = jnp.concatenate([jnp.sum(b * r, axis=0, keepdims=True), jnp.sum(k * r, axis=0, keepdims=True),
                                jnp.zeros((6, LANES), F32)], axis=0)
        x = jnp.concatenate([a, w, b, k, w * r, v, dots], axis=0)
        if tpack > 1:
            hi = x.astype(BF16)
            lo = (x - hi.astype(F32)).astype(BF16)
            x = _dot(jnp.concatenate([hi, lo], axis=1), spread_ref[...])
        for tl in range(tpack):
            t = tq * tpack + tl
            xt = x[:, tl * LANES:(tl + 1) * LANES]
            for qi in range(5):
                key_scr[t, qi] = xt[qi * RW_HEAD:(qi + 1) * RW_HEAD]
            vt = xt[5 * RW_HEAD:6 * RW_HEAD]
            vsel = vt[:SCAN_IHI]
            for g in range(1, SCAN_ILO):
                vsel = jnp.where(group_id == g, vt[g * SCAN_IHI:(g + 1) * SCAN_IHI], vsel)
            val_scr[t] = vsel
            dots_scr[t] = xt[6 * RW_HEAD:6 * RW_HEAD + 2]
        return carry

    n_prep = tb // tpack

    def step(t):
        def group(ih, c):
            s = s_scr[ih]
            sa = jnp.sum(s * key_scr[t, 0], axis=0, keepdims=True)
            sub = RW_HEAD // 8
            part_scr[pl.ds(pl.multiple_of(ih * 8, 8), 8), :] = jnp.sum(
                (s * key_scr[t, 4]).reshape(sub, 8, LANES), axis=0)
            row_scr[pl.ds(ih, 1), :] = sa
            s_scr[ih] = s * key_scr[t, 1] + sa * key_scr[t, 2] + val_scr[t, pl.ds(ih, 1), :] * key_scr[t, 3]
            return c

        lax.fori_loop(0, SCAN_IHI, group, 0, unroll=GROUP_UNROLL)
        sq = part_scr[pl.ds(0, SCAN_IHI, stride=8), :]
        for sl in range(1, 8):
            sq = sq + part_scr[pl.ds(sl, SCAN_IHI, stride=8), :]
        dots = dots_scr[t]
        y_ref[0, t] = sq + row_scr[...] * dots[0:1] + val_scr[t] * dots[1:2]

    def body(t, carry):
        step(t)
        return carry

    lax.fori_loop(0, n_prep, prepare, 0, unroll=2 if n_prep % 2 == 0 else 1)
    lax.fori_loop(0, tb, body, 0)

    @pl.when(tblk == pl.num_programs(1) - 1)
    def _():
        sf_ref[0] = s_scr[...]


def _wkv_scan(a, w, b, k, r, v, s0, t, tpack, tb=64):
    n_p = a.shape[0]
    tb = min(tb, t)
    keyspec = pl.BlockSpec((1, tb // tpack, RW_HEAD, LANES), lambda p, i: (p, i, 0, 0))
    valspec = pl.BlockSpec((1, tb, SCAN_IHI, LANES), lambda p, i: (p, i, 0, 0))
    stspec = pl.BlockSpec((1, SCAN_IHI, RW_HEAD, LANES), lambda p, i: (p, 0, 0, 0))
    out_lane = jnp.arange(SCAN_ILO * LANES)
    tok = out_lane // LANES
    seq = (out_lane % (LANES // SCAN_ILO)) // RW_HEADS
    src_lane = seq * (SCAN_ILO * RW_HEADS) + tok * RW_HEADS + out_lane % RW_HEADS
    spread = jnp.tile((jnp.arange(LANES)[:, None] == src_lane[None, :]).astype(BF16), (2, 1))
    return pl.pallas_call(
        functools.partial(_scan_kernel, tpack=tpack),
        grid=(n_p, t // tb),
        in_specs=[keyspec] * 6 + [stspec, _const_spec(spread.shape)],
        out_specs=[valspec, stspec],
        out_shape=[jax.ShapeDtypeStruct((n_p, t, SCAN_IHI, LANES), F32),
                   jax.ShapeDtypeStruct((n_p, SCAN_IHI, RW_HEAD, LANES), F32)],
        scratch_shapes=[pltpu.VMEM((SCAN_IHI, RW_HEAD, LANES), F32), pltpu.VMEM((tb, 5, RW_HEAD, LANES), F32),
                        pltpu.VMEM((tb, SCAN_IHI, LANES), F32), pltpu.VMEM((tb, 2, LANES), F32),
                        pltpu.VMEM((SCAN_IHI, LANES), F32), pltpu.VMEM((SCAN_IHI * 8, LANES), F32)],
        compiler_params=_params(("parallel", "arbitrary")),
        name="wkv_scan",
    )(a, w, b, k, r, v, s0, spread)


def _to_copied_lanes(x, bsz, t):
    g = bsz // SCAN_LANE_BATCH
    x = x.reshape(g, SCAN_LANE_BATCH, t, RW_HEADS, RW_HEAD).transpose(0, 2, 4, 1, 3)
    x = jnp.broadcast_to(x[:, :, :, None], (g, t, RW_HEAD, SCAN_ILO, SCAN_LANE_BATCH, RW_HEADS))
    return x.reshape(g, t, RW_HEAD, LANES)


def _from_value_lanes(y, bsz, t):
    g = bsz // SCAN_LANE_BATCH
    y = y.reshape(g, t, SCAN_IHI, SCAN_ILO, SCAN_LANE_BATCH, RW_HEADS).transpose(0, 4, 1, 5, 3, 2)
    return y.reshape(bsz * t, RW_WIDTH)


def _state_to_lanes(s):
    bsz = s.shape[0]
    g = bsz // SCAN_LANE_BATCH
    s = s.reshape(g, SCAN_LANE_BATCH, RW_HEADS, SCAN_ILO, SCAN_IHI, RW_HEAD).transpose(0, 4, 5, 3, 1, 2)
    return s.reshape(g, SCAN_IHI, RW_HEAD, LANES)


def _state_from_lanes(s, bsz):
    g = bsz // SCAN_LANE_BATCH
    s = s.reshape(g, SCAN_IHI, RW_HEAD, SCAN_ILO, SCAN_LANE_BATCH, RW_HEADS).transpose(0, 4, 5, 3, 1, 2)
    return s.reshape(bsz, RW_HEADS, RW_HEAD, RW_HEAD)


def _merge_kernel(x_ref, ys_ref, bonus_ref, g_ref, yb_ref, ga_ref, gb_ref,
                  lnw_ref, lnb_ref, ones_ref, wpa_ref, wpb_ref, wout_ref, o_ref):
    ones = ones_ref[...]
    y = ys_ref[...]
    inv_n = 1.0 / RW_HEAD
    mu = _seg_sum(y, ones) * inv_n
    d = y - mu
    var = _seg_sum(d * d, ones) * inv_n
    ya = d * lax.rsqrt(var + RW_GN_EPS) * lnw_ref[...] + lnb_ref[...]
    ya = (ya + bonus_ref[...]) * g_ref[...]
    merged = (ga_ref[...] * _dot(ya.astype(BF16), wpa_ref[...])
              + gb_ref[...] * _dot(yb_ref[...].astype(BF16), wpb_ref[...]))
    o_ref[...] = x_ref[...] + _dot(merged.astype(BF16), wout_ref[...])


def _merge(x, ys, bonus, g, yb, gates, p, tm=256):
    n, d = x.shape
    tm = min(tm, n)
    row = lambda i: (i, 0)
    half = pl.BlockSpec((tm, RW_WIDTH), row)
    return pl.pallas_call(
        _merge_kernel,
        grid=(n // tm,),
        in_specs=[pl.BlockSpec((tm, d), row), half, half, half, half,
                  pl.BlockSpec((tm, d), lambda i: (i, 0)), pl.BlockSpec((tm, d), lambda i: (i, 1)),
                  _const_spec((1, RW_WIDTH)), _const_spec((1, RW_WIDTH)), _const_spec((RW_WIDTH, RW_WIDTH)),
                  _const_spec((RW_WIDTH, d)), _const_spec((DA_V_WIDTH, d)), _const_spec((d, d))],
        out_specs=pl.BlockSpec((tm, d), row),
        out_shape=jax.ShapeDtypeStruct((n, d), F32),
        compiler_params=_params(("parallel",)),
        name="merge",
    )(x, ys, bonus, g, yb, gates, gates, p["ln_w"], p["ln_b"], p["ones"], p["wpa"], p["wpb"], p["wout"])


def _flash_kernel(lam_ref, q_ref, k_ref, v_ref, sub_ref, o_ref, m_scr, l_scr, acc_scr, *, tq, out_scale):
    qi = pl.program_id(2)
    q = q_ref[...]
    lane = lax.broadcasted_iota(jnp.int32, q.shape, 1)
    zero = jnp.zeros_like(q)
    q2 = jnp.concatenate([jnp.where(lane < DA_QK, q, zero), jnp.where(lane >= DA_QK, q, zero)], axis=0)
    m_scr[...] = jnp.full_like(m_scr, -jnp.inf)
    l_scr[...] = jnp.zeros_like(l_scr)
    acc_scr[...] = jnp.zeros_like(acc_scr)
    nt = (((1,), (1,)), ((), ()))
    reps = tq // LANES

    def block(j, diagonal):
        off = pl.multiple_of(j * tq, tq)
        kb = k_ref[pl.ds(off, tq), :]
        vb = v_ref[pl.ds(off, tq), :]
        s = lax.dot_general(q2, kb, nt, preferred_element_type=F32)
        if diagonal:
            rr = lax.broadcasted_iota(jnp.int32, s.shape, 0) & (tq - 1)
            cc = lax.broadcasted_iota(jnp.int32, s.shape, 1)
            s = jnp.where(cc <= rr, s, -jnp.inf)
        m_old = m_scr[...]
        m_new = jnp.maximum(m_old, jnp.max(s, axis=-1, keepdims=True))
        alpha = jnp.exp2(m_old - m_new)
        p = jnp.exp2(s - jnp.concatenate([m_new] * reps, axis=1))
        l_scr[...] = alpha * l_scr[...] + jnp.sum(p, axis=-1, keepdims=True)
        acc_scr[...] = alpha * acc_scr[...] + _dot(p.astype(BF16), vb)
        m_scr[...] = m_new

    def body(j, carry):
        block(j, False)
        return carry

    lax.fori_loop(0, qi, body, 0)
    block(qi, True)

    o = acc_scr[...] / l_scr[...]
    o = o[:tq] - lam_ref[...] * o[tq:]
    o_ref[...] = _rms(o, sub_ref[...]) * out_scale


def _flash(lam_row, qb, kb, vb, subln, bsz, s, out_scale, tq=512):
    tq = min(tq, s)
    q3 = qb.reshape(bsz, s, DA_QK_WIDTH)
    k3 = kb.reshape(bsz, s, DA_QK_WIDTH)
    v3 = vb.reshape(bsz, s, DA_V_WIDTH)
    out = pl.pallas_call(
        functools.partial(_flash_kernel, tq=tq, out_scale=out_scale),
        grid=(bsz, DA_HEADS, s // tq),
        in_specs=[_const_spec((1, LANES)),
                  pl.BlockSpec((None, tq, LANES), lambda b, h, i: (b, i, h)),
                  pl.BlockSpec((None, s, LANES), lambda b, h, i: (b, 0, h)),
                  pl.BlockSpec((None, s, DA_V), lambda b, h, i: (b, 0, h)),
                  pl.BlockSpec((1, DA_V), lambda b, h, i: (0, h))],
        out_specs=pl.BlockSpec((None, tq, DA_V), lambda b, h, i: (b, i, h)),
        out_shape=jax.ShapeDtypeStruct((bsz, s, DA_V_WIDTH), F32),
        scratch_shapes=[pltpu.VMEM((2 * tq, LANES), F32), pltpu.VMEM((2 * tq, LANES), F32),
                        pltpu.VMEM((2 * tq, DA_V), F32)],
        compiler_params=_params(("parallel", "parallel", "arbitrary")),
        name="flash_diff",
    )(lam_row, q3, k3, v3, subln)
    return out.reshape(bsz * s, DA_V_WIDTH)


def _decode_kernel(pt_ref, lam_ref, q_ref, qd_ref, kn_ref, vn_ref, sub_ref, *rest, pages, out_scale):
    k_refs = rest[:pages]
    v_refs = rest[pages:2 * pages]
    o_ref, m_scr, l_scr, acc_scr = rest[2 * pages:]
    j = pl.program_id(1)
    own_head = lax.broadcasted_iota(jnp.int32, acc_scr.shape, 0) >> 1

    @pl.when(j == 0)
    def _():
        m_scr[...] = jnp.sum(q_ref[...] * kn_ref[...], axis=-1, keepdims=True)
        l_scr[...] = jnp.ones_like(l_scr)
        acc_scr[...] = vn_ref[...]

    qd = qd_ref[...]
    s = jnp.concatenate([_dot(qd, k_refs[u][...].astype(BF16)) for u in range(pages)], axis=1)
    m_old = m_scr[...]
    m_new = jnp.maximum(m_old, jnp.max(s, axis=-1, keepdims=True))
    alpha = jnp.exp2(m_old - m_new)
    p = jnp.exp2(s - m_new)
    l_scr[...] = alpha * l_scr[...] + jnp.sum(p, axis=-1, keepdims=True)
    m_scr[...] = m_new
    pv = jnp.zeros(acc_scr.shape, F32)
    for u in range(pages):
        pu = p[:, u * PAGE_SIZE:(u + 1) * PAGE_SIZE].astype(BF16)
        for hv in range(DA_HEADS):
            vh = v_refs[u][pl.ds(hv, PAGE_SIZE, stride=DA_HEADS), :].astype(BF16)
            pv = pv + jnp.where(own_head == hv, _dot(pu, vh), 0.0)
    acc_scr[...] = alpha * acc_scr[...] + pv

    @pl.when(j == pl.num_programs(1) - 1)
    def _():
        o8 = acc_scr[...] / l_scr[...]
        o = o8 - lam_ref[...] * pltpu.roll(o8, 2 * DA_HEADS - 1, 0)
        o_ref[...] = _rms(o, sub_ref[...]) * out_scale


def _decode_attn(lam_row, q8, k_new8, v_new8, subln8, cache_kt, cache_v, page_table, out_scale, pages=16):
    bsz, n_pages = page_table.shape
    pages = min(pages, n_pages)
    pt = page_table.reshape(-1)
    n_heads = 2 * DA_HEADS

    def page_map(u):
        return lambda b, j, pt_ref: (pt_ref[b * n_pages + j * pages + u], 0, 0)

    q_diag = (q8[:, :, None, :] * jnp.eye(n_heads, dtype=F32)[None, :, :, None]).reshape(bsz, n_heads, -1)
    seq = lambda b, j, pt_ref: (b, 0, 0)
    fixed = lambda b, j, pt_ref: (0, 0)
    in_specs = [pl.BlockSpec((1, LANES), fixed),
                pl.BlockSpec((None, n_heads, DA_QK), seq),
                pl.BlockSpec((None, n_heads, n_heads * DA_QK), seq),
                pl.BlockSpec((None, n_heads, DA_QK), seq),
                pl.BlockSpec((None, n_heads, DA_V), seq),
                pl.BlockSpec((n_heads, DA_V), fixed)]
    in_specs += [pl.BlockSpec((None, n_heads * DA_QK, PAGE_SIZE), page_map(u)) for u in range(pages)]
    in_specs += [pl.BlockSpec((None, PAGE_SIZE * DA_HEADS, DA_V), page_map(u)) for u in range(pages)]
    grid_spec = pltpu.PrefetchScalarGridSpec(
        num_scalar_prefetch=1,
        grid=(bsz, n_pages // pages),
        in_specs=in_specs,
        out_specs=pl.BlockSpec((None, n_heads, DA_V), seq),
        scratch_shapes=[pltpu.VMEM((n_heads, 1), F32), pltpu.VMEM((n_heads, 1), F32),
                        pltpu.VMEM((n_heads, DA_V), F32)])
    return pl.pallas_call(
        functools.partial(_decode_kernel, pages=pages, out_scale=out_scale),
        grid_spec=grid_spec,
        out_shape=jax.ShapeDtypeStruct((bsz, n_heads, DA_V), F32),
        compiler_params=_params(("parallel", "arbitrary")),
        name="decode_attn",
    )(pt, lam_row, q8, q_diag.astype(BF16), k_new8, v_new8, subln8,
      *([cache_kt] * pages), *([cache_v] * pages))


def _rope_tables(pos):
    half = DA_QK // 2
    inv = ROPE_THETA ** (-jnp.arange(half, dtype=F32) / half)
    ang = pos.astype(F32)[:, None] * inv[None, :]
    cos, sin = jnp.cos(ang), jnp.sin(ang)
    reps = LANES // DA_QK
    return jnp.tile(jnp.concatenate([cos, cos], -1), (1, reps)), jnp.tile(jnp.concatenate([-sin, sin], -1), (1, reps))


def _layer_params(l, ffn1_norm, ffn1_up, ffn1_down, mix_norm, w_in, gate_bias, rw_mu, rw_w0, rw_w2, rw_a0,
                  rw_a2, rw_g2, rw_k_k, rw_k_a, rw_r_k, rw_ln_w, rw_ln_b, da_subln, w_proj_a, w_proj_b, w_out,
                  ffn2_norm, ffn2_up, ffn2_down):
    def ffn_w(up, down):
        d, two_ff = up.shape
        ff = two_ff // 2
        tf = 256
        nc = ff // tf
        split = lambda w: w.reshape(d, nc, tf).transpose(1, 0, 2).astype(BF16)
        return split(up[:, :ff]), split(up[:, ff:]), down.reshape(nc, tf, d).astype(BF16)

    row = lambda v: v.reshape(1, -1)
    head_id = jnp.arange(RW_WIDTH) // RW_HEAD
    zeros = jnp.zeros((DECAY_LORA, RW_WIDTH), F32)
    return dict(
        ffn1=(row(ffn1_norm[l]),) + ffn_w(ffn1_up[l], ffn1_down[l]),
        ffn2=(row(ffn2_norm[l]),) + ffn_w(ffn2_up[l], ffn2_down[l]),
        mix_norm=row(mix_norm[l]), w_in=w_in[l].astype(BF16), gate_bias=row(gate_bias[l]),
        mu=row(rw_mu[l]), w0=row(rw_w0[l]), a0=row(rw_a0[l]), k_k=row(rw_k_k[l]), k_a=row(rw_k_a[l]),
        r_k=row(rw_r_k[l]),
        w2=jnp.concatenate([rw_w2[l], zeros], 0).astype(BF16),
        a2=jnp.concatenate([zeros, rw_a2[l]], 0).astype(BF16),
        g2=rw_g2[l].astype(BF16),
        ones=(head_id[:, None] == head_id[None, :]).astype(BF16),
        ln_w=row(rw_ln_w[l]), ln_b=row(rw_ln_b[l]), subln=row(da_subln[l]),
        wpa=w_proj_a[l].astype(BF16), wpb=w_proj_b[l].astype(BF16), wout=w_out[l].astype(BF16))


def _trunk_front(x, p, cos_t, sin_t, z_last, bsz, t):
    x1 = _ffn(x, *p["ffn1"])
    z, qb, k, v, kb, vb, gates = _mixproj(x1, p["mix_norm"], p["w_in"], p["gate_bias"], cos_t, sin_t, bsz, t)
    z3 = z.reshape(bsz, t, RW_COLS)
    if bsz == SCAN_LANE_BATCH and t % SCAN_ILO == 0:
        tpack = SCAN_ILO
        ops, g, bonus = _rwprep_lanes(z3, z_last, p)
        ops = [u[None] for u in ops]
    else:
        assert t == 1
        tpack = 1
        r, dec, kmod, vv, an, bn, g, bonus = _rwprep(z, z_last, p)
        ops = [_to_copied_lanes(u, bsz, t) for u in (an, dec, bn, kmod, r, vv)]
    return x1, z3, qb, k, v, kb, vb, gates, (ops, tpack), g, bonus


def _rwkv_scan(rw, wkv0, bsz, t):
    ops, tpack = rw
    y, s_fin = _wkv_scan(*ops, _state_to_lanes(wkv0), t, tpack)
    return _from_value_lanes(y, bsz, t), _state_from_lanes(s_fin, bsz)


def kernel(x_prompt, x_sample, cache_k, cache_v, state_wkv, state_shift, page_table, ffn1_norm, ffn1_up, ffn1_down, mix_norm, w_in, gate_bias, rw_mu, rw_w0, rw_w2, rw_a0, rw_a2, rw_g2, rw_k_k, rw_k_a, rw_r_k, rw_ln_w, rw_ln_b, da_lq1, da_lk1, da_lq2, da_lk2, da_subln, w_proj_a, w_proj_b, w_out, ffn2_norm, ffn2_up, ffn2_down, final_norm):
    bp, sp, d = x_prompt.shape
    bs, ts, _ = x_sample.shape
    depth = ffn1_norm.shape[0]
    assert ts == 1 and bp % SCAN_LANE_BATCH == 0 and bs % SCAN_LANE_BATCH == 0
    assert cache_k.shape[2] == PAGE_SIZE == LANES
    n_past = page_table.shape[1] * PAGE_SIZE
    cos_p, sin_p = _rope_tables(jnp.arange(sp, dtype=jnp.int32))
    cos_s, sin_s = _rope_tables(jnp.full((bs,), n_past, dtype=jnp.int32))
    final_row = final_norm.reshape(1, d)

    hp = x_prompt.reshape(bp * sp, d)
    hs = x_sample.reshape(bs * ts, d)
    outs = [[] for _ in range(8)]
    for l in range(depth):
        p = _layer_params(l, ffn1_norm, ffn1_up, ffn1_down, mix_norm, w_in, gate_bias, rw_mu, rw_w0, rw_w2,
                          rw_a0, rw_a2, rw_g2, rw_k_k, rw_k_a, rw_r_k, rw_ln_w, rw_ln_b, da_subln, w_proj_a,
                          w_proj_b, w_out, ffn2_norm, ffn2_up, ffn2_down)
        lam_init = 0.8 - 0.6 * math.exp(-0.3 * l)
        lam = (jnp.exp(jnp.sum(da_lq1[l] * da_lk1[l])) - jnp.exp(jnp.sum(da_lq2[l] * da_lk2[l])) + lam_init)
        lam_row = jnp.full((1, LANES), lam, F32)
        out_scale = 1.0 - lam_init
        last = l == depth - 1

        x1, z3, qb, k, v, kb, vb, gates, rw, g, bonus = _trunk_front(
            hp, p, cos_p, sin_p, jnp.zeros((bp, RW_COLS), F32), bp, sp)
        ys, wkv_p = _rwkv_scan(rw, jnp.zeros((bp, RW_HEADS, RW_HEAD, RW_HEAD), F32), bp, sp)
        yb = _flash(lam_row, qb, kb, vb, p["subln"], bp, sp, out_scale)
        x2 = _merge(x1, ys, bonus, g, yb, gates, p)
        hp = _ffn(x2, *p["ffn2"], final_g=final_row if last else None)
        outs[0].append(k)
        outs[1].append(v)
        outs[2].append(wkv_p)
        outs[3].append(z3[:, -1])

        x1, z3, qb, k, v, kb, vb, gates, rw, g, bonus = _trunk_front(
            hs, p, cos_s, sin_s, state_shift[l], bs, ts)
        ys, wkv_s = _rwkv_scan(rw, state_wkv[l], bs, ts)
        q8 = qb.astype(F32).reshape(bs, 2 * DA_HEADS, DA_QK)
        k8 = k.reshape(bs, 2 * DA_HEADS, DA_QK)
        v8 = jnp.repeat(v.reshape(bs, DA_HEADS, DA_V), 2, axis=1)
        sub8 = jnp.repeat(p["subln"].reshape(DA_HEADS, DA_V), 2, axis=0)
        cache_vf = cache_v[l].reshape(cache_v.shape[1], PAGE_SIZE * DA_HEADS, DA_V)
        cache_kt = jnp.transpose(cache_k[l], (0, 2, 3, 1)).reshape(cache_k.shape[1], DA_QK_WIDTH, PAGE_SIZE)
        o8 = _decode_attn(lam_row, q8, k8, v8, sub8, cache_kt, cache_vf, page_table, out_scale)
        yb = o8[:, ::2].reshape(bs, DA_V_WIDTH)
        x2 = _merge(x1, ys, bonus, g, yb, gates, p)
        hs = _ffn(x2, *p["ffn2"], final_g=final_row if last else None)
        outs[4].append(k)
        outs[5].append(v)
        outs[6].append(wkv_s)
        outs[7].append(z3[:, -1])

    st = [jnp.stack(o) for o in outs]
    return (hp.reshape(bp, sp, d), hs.reshape(bs, ts, d), st[0], st[1], st[2], st[3], st[4], st[5], st[6], st[7])
```

```python
import functools
import math

import jax
import jax.numpy as jnp
from jax import lax
from jax.experimental import pallas as pl
from jax.experimental.pallas import tpu as pltpu

F32 = jnp.float32
BF16 = jnp.bfloat16

NORM_EPS = 1e-6
ROPE_THETA = 10000.0
PAGE_SIZE = 128

RW_HEADS = 8
RW_HEAD = 64
RW_WIDTH = RW_HEADS * RW_HEAD
DECAY_LORA = 64
AAA_LORA = 64
GATE_LORA = 128
RW_COLS = 3 * RW_WIDTH + DECAY_LORA + AAA_LORA + GATE_LORA
RW_GN_EPS = 1e-5 * RW_HEAD * RW_HEAD
DA_HEADS = 4
DA_QK = 64
DA_V = 2 * DA_QK
DA_QK_WIDTH = 2 * DA_HEADS * DA_QK
DA_V_WIDTH = DA_HEADS * DA_V
Q_SCALE = DA_QK ** -0.5 * math.log2(math.e)

LANES = 128
SCAN_LANE_BATCH = 4
SCAN_ILO = LANES // (SCAN_LANE_BATCH * RW_HEADS)
SCAN_IHI = RW_HEAD // SCAN_ILO
GROUP_UNROLL = 16
VMEM_LIMIT = 56 * 1024 * 1024


def _const_spec(shape):
    nd = len(shape)
    return pl.BlockSpec(shape, lambda *_: (0,) * nd, pipeline_mode=pl.Buffered(1))


def _params(sem):
    return pltpu.CompilerParams(dimension_semantics=sem, vmem_limit_bytes=VMEM_LIMIT)


def _rms(x, g):
    return x * lax.rsqrt(jnp.mean(x * x, axis=-1, keepdims=True) + NORM_EPS) * g


def _dot(a, b):
    return jnp.dot(a, b, preferred_element_type=F32)


def _seg_sum(x, ones_bf):
    hi = x.astype(BF16)
    lo = (x - hi.astype(F32)).astype(BF16)
    return _dot(hi, ones_bf) + _dot(lo, ones_bf)


def _ffn_kernel(x_ref, g_ref, wg_ref, wu_ref, wd_ref, *rest, final):
    if final:
        gf_ref, o_ref, h_scr, acc_scr = rest
    else:
        o_ref, h_scr, acc_scr = rest
    x = x_ref[...]
    h_scr[...] = _rms(x, g_ref[...]).astype(BF16)
    acc_scr[...] = jnp.zeros_like(acc_scr)

    def body(c, carry):
        hb = h_scr[...]
        gate = _dot(hb, wg_ref[c])
        up = _dot(hb, wu_ref[c])
        act = (gate * jax.nn.sigmoid(gate) * up).astype(BF16)
        acc_scr[...] += _dot(act, wd_ref[c])
        return carry

    lax.fori_loop(0, wg_ref.shape[0], body, 0, unroll=True)
    y = x_ref[...] + 0.5 * acc_scr[...]
    if final:
        y = _rms(y, gf_ref[...])
    o_ref[...] = y


def _ffn(x, norm_g, wg, wu, wd, final_g=None, tm=1024):
    n, d = x.shape
    nc, _, tf = wg.shape
    tm = min(tm, n)
    final = final_g is not None
    in_specs = [pl.BlockSpec((tm, d), lambda i: (i, 0)), _const_spec((1, d)),
                _const_spec((nc, d, tf)), _const_spec((nc, d, tf)), _const_spec((nc, tf, d))]
    args = [x, norm_g, wg, wu, wd]
    if final:
        in_specs.append(_const_spec((1, d)))
        args.append(final_g)
    return pl.pallas_call(
        functools.partial(_ffn_kernel, final=final),
        grid=(n // tm,),
        in_specs=in_specs,
        out_specs=pl.BlockSpec((tm, d), lambda i: (i, 0)),
        out_shape=jax.ShapeDtypeStruct((n, d), F32),
        scratch_shapes=[pltpu.VMEM((tm, d), BF16), pltpu.VMEM((tm, d), F32)],
        compiler_params=_params(("parallel",)),
        name="ffn_final" if final else "ffn",
    )(*args)


def _mixproj_kernel(x_ref, g_ref, w_ref, gb_ref, cos_ref, sin_ref,
                    z_ref, q_ref, k_ref, v_ref, kb_ref, vb_ref, gate_ref, *, result_layout):
    hb = _rms(x_ref[...], g_ref[...]).astype(BF16)
    for c in range(0, RW_COLS, 256):
        z_ref[:, c:c + 256] = _dot(hb, w_ref[:, c:c + 256])

    cos = cos_ref[...]
    sin = sin_ref[...]
    lane = lax.broadcasted_iota(jnp.int32, cos.shape, 1)
    first_half = (lane & (DA_QK // 2)) == 0

    def rope(xx):
        swapped = jnp.where(first_half, pltpu.roll(xx, LANES - DA_QK // 2, 1), pltpu.roll(xx, DA_QK // 2, 1))
        return xx * cos + swapped * sin

    q_off = RW_COLS
    k_off = q_off + DA_QK_WIDTH
    v_off = k_off + DA_QK_WIDTH
    g_off = v_off + DA_V_WIDTH
    q = _dot(hb, w_ref[:, q_off:q_off + DA_QK_WIDTH])
    k = _dot(hb, w_ref[:, k_off:k_off + DA_QK_WIDTH])
    for c in range(0, DA_QK_WIDTH, LANES):
        q_ref[:, c:c + LANES] = (rope(q[:, c:c + LANES]) * Q_SCALE).astype(BF16)
        kr = rope(k[:, c:c + LANES])
        kb_ref[:, c:c + LANES] = kr.astype(BF16)
        if result_layout:
            kt = kr.T
            head = 2 * (c // LANES)
            k_ref[head] = kt[:DA_QK]
            k_ref[head + 1] = kt[DA_QK:]
        else:
            k_ref[:, c:c + LANES] = kr
    v = _dot(hb, w_ref[:, v_off:v_off + DA_V_WIDTH])
    vb_ref[...] = v.astype(BF16)
    if result_layout:
        for hv in range(DA_HEADS):
            v_ref[pl.ds(hv, v.shape[0], stride=DA_HEADS), :] = v[:, hv * DA_V:(hv + 1) * DA_V]
    else:
        v_ref[...] = v
    for c in range(0, gate_ref.shape[1], 512):
        gate_ref[:, c:c + 512] = jax.nn.sigmoid(
            _dot(hb, w_ref[:, g_off + c:g_off + c + 512]) + gb_ref[:, c:c + 512])


def _mixproj(x, norm_g, w_in, gate_bias, cos_t, sin_t, bsz, t, tm=256):
    n, d = x.shape
    in_cols = w_in.shape[1]
    gate_cols = gate_bias.shape[1]
    tm = min(tm, n)
    result_layout = t % tm == 0 and tm % LANES == 0
    rope_blocks = max(t // tm, 1)
    row = lambda i: (i, 0)
    rope_map = lambda i: (i % rope_blocks, 0)
    outs = [(RW_COLS, F32), (DA_QK_WIDTH, BF16), (DA_QK_WIDTH, F32), (DA_V_WIDTH, F32),
            (DA_QK_WIDTH, BF16), (DA_V_WIDTH, BF16), (gate_cols, F32)]
    out_specs = [pl.BlockSpec((tm, w), row) for w, _ in outs]
    out_shape = [jax.ShapeDtypeStruct((n, w), dt) for w, dt in outs]
    if result_layout:
        out_specs[2] = pl.BlockSpec((None, 2 * DA_HEADS, DA_QK, tm),
                                    lambda i: (i // rope_blocks, 0, 0, i % rope_blocks))
        out_shape[2] = jax.ShapeDtypeStruct((bsz, 2 * DA_HEADS, DA_QK, t), F32)
        out_specs[3] = pl.BlockSpec((tm * DA_HEADS, DA_V), row)
        out_shape[3] = jax.ShapeDtypeStruct((n * DA_HEADS, DA_V), F32)
    res = list(pl.pallas_call(
        functools.partial(_mixproj_kernel, result_layout=result_layout),
        grid=(n // tm,),
        in_specs=[pl.BlockSpec((tm, d), row), _const_spec((1, d)), _const_spec((d, in_cols)),
                  _const_spec((1, gate_cols)),
                  pl.BlockSpec((tm, LANES), rope_map), pl.BlockSpec((tm, LANES), rope_map)],
        out_specs=out_specs,
        out_shape=out_shape,
        compiler_params=_params(("parallel",)),
        name="mixproj",
    )(x, norm_g, w_in, gate_bias, cos_t, sin_t))
    if result_layout:
        res[2] = jnp.transpose(res[2], (0, 3, 1, 2))
    else:
        res[2] = res[2].reshape(bsz, t, 2 * DA_HEADS, DA_QK)
    res[3] = res[3].reshape(bsz, t, DA_HEADS, DA_V)
    return res


def _rw_vectors(z, zp, mu_ref, w0_ref, a0_ref, kk_ref, ka_ref, rk_ref, w2_ref, a2_ref, g2_ref, ones_ref):
    zm = z + (zp - z) * mu_ref[...]
    o1 = RW_WIDTH
    r = zm[:, :o1]
    k = zm[:, o1:2 * o1]
    v = zm[:, 2 * o1:3 * o1]
    lora_in = zm[:, 3 * o1:3 * o1 + LANES]
    gd = zm[:, 3 * o1 + LANES:]
    ones = ones_ref[...]

    x = w0_ref[...] + _dot(jnp.tanh(lora_in).astype(BF16), w2_ref[...])
    neg = -x
    softplus = jnp.maximum(neg, 0.0) + jnp.log1p(jnp.exp(-jnp.abs(neg)))
    dec = jnp.exp(-jnp.exp(-softplus - 0.5))
    a = jax.nn.sigmoid(a0_ref[...] + _dot(lora_in.astype(BF16), a2_ref[...]))
    g = _dot(jax.nn.sigmoid(gd).astype(BF16), g2_ref[...])

    kk = k * kk_ref[...]
    norm = jnp.sqrt(_seg_sum(kk * kk, ones))
    kk = kk / jnp.maximum(norm, 1e-12)
    kmod = k * (1.0 + (a - 1.0) * ka_ref[...])
    bonus = _seg_sum(r * kmod * rk_ref[...], ones) * v
    return r, dec, kmod, v, -kk, kk * a, g, bonus


def _rwprep_kernel(z_ref, zp_ref, *refs):
    outs = _rw_vectors(z_ref[...], zp_ref[...], *refs[:10])
    for o_ref, val in zip(refs[10:], outs):
        o_ref[...] = val


def _rwprep_lanes_kernel(z_ref, zlast_ref, *refs, tt):
    params = refs[:10]
    lane_refs = refs[10:16]
    g_ref, bonus_ref, carry_scr, th_scr = refs[16:]

    @pl.when(pl.program_id(0) == 0)
    def _():
        carry_scr[...] = zlast_ref[...]

    first_row = lax.broadcasted_iota(jnp.int32, (tt, RW_COLS), 0) == 0
    rows = SCAN_ILO * RW_HEADS
    zs, zps = [], []
    for b in range(SCAN_LANE_BATCH):
        z = z_ref[b]
        zs.append(z)
        zps.append(jnp.where(first_row, carry_scr[pl.ds(b, 1), :], pltpu.roll(z, 1, 0)))
        carry_scr[pl.ds(b, 1), :] = z[tt - 1:tt, :]
    r, dec, kmod, v, an, bn, g, bonus = _rw_vectors(jnp.concatenate(zs, axis=0), jnp.concatenate(zps, axis=0), *params)
    for b in range(SCAN_LANE_BATCH):
        seq_rows = slice(b * tt, (b + 1) * tt)
        g_ref[b] = g[seq_rows]
        bonus_ref[b] = bonus[seq_rows]
        for idx, val in enumerate((an, dec, bn, kmod, r, v)):
            for h in range(RW_HEADS):
                th_scr[idx, b, pl.ds(h, tt, stride=RW_HEADS), :] = val[seq_rows, h * RW_HEAD:(h + 1) * RW_HEAD]
    for idx, o_ref in enumerate(lane_refs):
        for tq in range(tt // SCAN_ILO):
            tile = jnp.concatenate([th_scr[idx, b, rows * tq:rows * (tq + 1), :] for b in range(SCAN_LANE_BATCH)],
                                   axis=0)
            o_ref[tq] = tile.T


def _rwprep_lanes(z3, z_last, p, tt=64):
    bsz, t, _ = z3.shape
    tt = min(tt, t)
    vec = _const_spec((1, RW_WIDTH))
    lanes_spec = pl.BlockSpec((tt // SCAN_ILO, RW_HEAD, LANES), lambda i: (i, 0, 0))
    rows_spec = pl.BlockSpec((bsz, tt, RW_WIDTH), lambda i: (0, i, 0))
    outs = pl.pallas_call(
        functools.partial(_rwprep_lanes_kernel, tt=tt),
        grid=(t // tt,),
        in_specs=[pl.BlockSpec((bsz, tt, RW_COLS), lambda i: (0, i, 0)), _const_spec((bsz, RW_COLS)),
                  _const_spec((1, RW_COLS)), vec, vec, vec, vec, vec,
                  _const_spec((LANES, RW_WIDTH)), _const_spec((LANES, RW_WIDTH)),
                  _const_spec((GATE_LORA, RW_WIDTH)), _const_spec((RW_WIDTH, RW_WIDTH))],
        out_specs=[lanes_spec] * 6 + [rows_spec] * 2,
        out_shape=[jax.ShapeDtypeStruct((t // SCAN_ILO, RW_HEAD, LANES), F32)] * 6
        + [jax.ShapeDtypeStruct((bsz, t, RW_WIDTH), F32)] * 2,
        scratch_shapes=[pltpu.VMEM((bsz, RW_COLS), F32),
                        pltpu.VMEM((6, bsz, tt * RW_HEADS, RW_HEAD), F32)],
        compiler_params=_params(("arbitrary",)),
        name="rwprep_lanes",
    )(z3, z_last, p["mu"], p["w0"], p["a0"], p["k_k"], p["k_a"], p["r_k"], p["w2"], p["a2"], p["g2"], p["ones"])
    return outs[:6], outs[6].reshape(bsz * t, RW_WIDTH), outs[7].reshape(bsz * t, RW_WIDTH)


def _rwprep(z, zprev, p, tm=256):
    n = z.shape[0]
    tm = min(tm, n)
    row = lambda i: (i, 0)
    vec = _const_spec((1, RW_WIDTH))
    return pl.pallas_call(
        _rwprep_kernel,
        grid=(n // tm,),
        in_specs=[pl.BlockSpec((tm, RW_COLS), row), pl.BlockSpec((tm, RW_COLS), row),
                  _const_spec((1, RW_COLS)), vec, vec, vec, vec, vec,
                  _const_spec((LANES, RW_WIDTH)), _const_spec((LANES, RW_WIDTH)),
                  _const_spec((GATE_LORA, RW_WIDTH)), _const_spec((RW_WIDTH, RW_WIDTH))],
        out_specs=[pl.BlockSpec((tm, RW_WIDTH), row)] * 8,
        out_shape=[jax.ShapeDtypeStruct((n, RW_WIDTH), F32)] * 8,
        compiler_params=_params(("parallel",)),
        name="rwprep",
    )(z, zprev, p["mu"], p["w0"], p["a0"], p["k_k"], p["k_a"], p["r_k"],
      p["w2"], p["a2"], p["g2"], p["ones"])


def _scan_kernel(a_ref, w_ref, b_ref, k_ref, r_ref, v_ref, s0_ref, spread_ref, y_ref, sf_ref,
                 s_scr, key_scr, val_scr, dots_scr, row_scr, part_scr, *, tpack):
    tblk = pl.program_id(1)
    tb = y_ref.shape[1]

    @pl.when(tblk == 0)
    def _():
        s_scr[...] = s0_ref[0]

    group_id = lax.broadcasted_iota(jnp.int32, (SCAN_IHI, LANES), 1) // (LANES // SCAN_ILO)

    def prepare(tq, carry):
        a, w, b, k, r, v = (ref[0, tq] for ref in (a_ref, w_ref, b_ref, k_ref, r_ref, v_ref))
        dots = jnp.concatenate([jnp.sum(b * r, axis=0, keepdims=True), jnp.sum(k * r, axis=0, keepdims=True),
                                jnp.zeros((6, LANES), F32)], axis=0)
        x = jnp.concatenate([a, w, b, k, w * r, v, dots], axis=0)
        if tpack > 1:
            hi = x.astype(BF16)
            lo = (x - hi.astype(F32)).astype(BF16)
            x = _dot(jnp.concatenate([hi, lo], axis=1), spread_ref[...])
        for tl in range(tpack):
            t = tq * tpack + tl
            xt = x[:, tl * LANES:(tl + 1) * LANES]
            for qi in range(5):
                key_scr[t, qi] = xt[qi * RW_HEAD:(qi + 1) * RW_HEAD]
            vt = xt[5 * RW_HEAD:6 * RW_HEAD]
            vsel = vt[:SCAN_IHI]
            for g in range(1, SCAN_ILO):
                vsel = jnp.where(group_id == g, vt[g * SCAN_IHI:(g + 1) * SCAN_IHI], vsel)
            val_scr[t] = vsel
            dots_scr[t] = xt[6 * RW_HEAD:6 * RW_HEAD + 2]
        return carry

    n_prep = tb // tpack

    def step(t):
        def group(ih, c):
            s = s_scr[ih]
            sa = jnp.sum(s * key_scr[t, 0], axis=0, keepdims=True)
            sub = RW_HEAD // 8
            part_scr[pl.ds(pl.multiple_of(ih * 8, 8), 8), :] = jnp.sum(
                (s * key_scr[t, 4]).reshape(sub, 8, LANES), axis=0)
            row_scr[pl.ds(ih, 1), :] = sa
            s_scr[ih] = s * key_scr[t, 1] + sa * key_scr[t, 2] + val_scr[t, pl.ds(ih, 1), :] * key_scr[t, 3]
            return c

        lax.fori_loop(0, SCAN_IHI, group, 0, unroll=GROUP_UNROLL)
        sq = part_scr[pl.ds(0, SCAN_IHI, stride=8), :]
        for sl in range(1, 8):
            sq = sq + part_scr[pl.ds(sl, SCAN_IHI, stride=8), :]
        dots = dots_scr[t]
        y_ref[0, t] = sq + row_scr[...] * dots[0:1] + val_scr[t] * dots[1:2]

    def body(t, carry):
        step(t)
        return carry

    lax.fori_loop(0, n_prep, prepare, 0, unroll=8 if n_prep % 8 == 0 else 1)
    lax.fori_loop(0, tb, body, 0)

    @pl.when(tblk == pl.num_programs(1) - 1)
    def _():
        sf_ref[0] = s_scr[...]


def _wkv_scan(a, w, b, k, r, v, s0, t, tpack, tb=64):
    n_p = a.shape[0]
    tb = min(tb, t)
    keyspec = pl.BlockSpec((1, tb // tpack, RW_HEAD, LANES), lambda p, i: (p, i, 0, 0))
    valspec = pl.BlockSpec((1, tb, SCAN_IHI, LANES), lambda p, i: (p, i, 0, 0))
    stspec = pl.BlockSpec((1, SCAN_IHI, RW_HEAD, LANES), lambda p, i: (p, 0, 0, 0))
    out_lane = jnp.arange(SCAN_ILO * LANES)
    tok = out_lane // LANES
    seq = (out_lane % (LANES // SCAN_ILO)) // RW_HEADS
    src_lane = seq * (SCAN_ILO * RW_HEADS) + tok * RW_HEADS + out_lane % RW_HEADS
    spread = jnp.tile((jnp.arange(LANES)[:, None] == src_lane[None, :]).astype(BF16), (2, 1))
    return pl.pallas_call(
        functools.partial(_scan_kernel, tpack=tpack),
        grid=(n_p, t // tb),
        in_specs=[keyspec] * 6 + [stspec, _const_spec(spread.shape)],
        out_specs=[valspec, stspec],
        out_shape=[jax.ShapeDtypeStruct((n_p, t, SCAN_IHI, LANES), F32),
                   jax.ShapeDtypeStruct((n_p, SCAN_IHI, RW_HEAD, LANES), F32)],
        scratch_shapes=[pltpu.VMEM((SCAN_IHI, RW_HEAD, LANES), F32), pltpu.VMEM((tb, 5, RW_HEAD, LANES), F32),
                        pltpu.VMEM((tb, SCAN_IHI, LANES), F32), pltpu.VMEM((tb, 2, LANES), F32),
                        pltpu.VMEM((SCAN_IHI, LANES), F32), pltpu.VMEM((SCAN_IHI * 8, LANES), F32)],
        compiler_params=_params(("parallel", "arbitrary")),
        name="wkv_scan",
    )(a, w, b, k, r, v, s0, spread)


def _to_copied_lanes(x, bsz, t):
    g = bsz // SCAN_LANE_BATCH
    x = x.reshape(g, SCAN_LANE_BATCH, t, RW_HEADS, RW_HEAD).transpose(0, 2, 4, 1, 3)
    x = jnp.broadcast_to(x[:, :, :, None], (g, t, RW_HEAD, SCAN_ILO, SCAN_LANE_BATCH, RW_HEADS))
    return x.reshape(g, t, RW_HEAD, LANES)


def _from_value_lanes(y, bsz, t):
    g = bsz // SCAN_LANE_BATCH
    y = y.reshape(g, t, SCAN_IHI, SCAN_ILO, SCAN_LANE_BATCH, RW_HEADS).transpose(0, 4, 1, 5, 3, 2)
    return y.reshape(bsz * t, RW_WIDTH)


def _state_to_lanes(s):
    bsz = s.shape[0]
    g = bsz // SCAN_LANE_BATCH
    s = s.reshape(g, SCAN_LANE_BATCH, RW_HEADS, SCAN_ILO, SCAN_IHI, RW_HEAD).transpose(0, 4, 5, 3, 1, 2)
    return s.reshape(g, SCAN_IHI, RW_HEAD, LANES)


def _state_from_lanes(s, bsz):
    g = bsz // SCAN_LANE_BATCH
    s = s.reshape(g, SCAN_IHI, RW_HEAD, SCAN_ILO, SCAN_LANE_BATCH, RW_HEADS).transpose(0, 4, 5, 3, 1, 2)
    return s.reshape(bsz, RW_HEADS, RW_HEAD, RW_HEAD)


def _merge_kernel(x_ref, ys_ref, bonus_ref, g_ref, yb_ref, ga_ref, gb_ref,
                  lnw_ref, lnb_ref, ones_ref, wpa_ref, wpb_ref, wout_ref, o_ref):
    ones = ones_ref[...]
    y = ys_ref[...]
    inv_n = 1.0 / RW_HEAD
    mu = _seg_sum(y, ones) * inv_n
    d = y - mu
    var = _seg_sum(d * d, ones) * inv_n
    ya = d * lax.rsqrt(var + RW_GN_EPS) * lnw_ref[...] + lnb_ref[...]
    ya = (ya + bonus_ref[...]) * g_ref[...]
    merged = (ga_ref[...] * _dot(ya.astype(BF16), wpa_ref[...])
              + gb_ref[...] * _dot(yb_ref[...].astype(BF16), wpb_ref[...]))
    o_ref[...] = x_ref[...] + _dot(merged.astype(BF16), wout_ref[...])


def _merge(x, ys, bonus, g, yb, gates, p, tm=256):
    n, d = x.shape
    tm = min(tm, n)
    row = lambda i: (i, 0)
    half = pl.BlockSpec((tm, RW_WIDTH), row)
    return pl.pallas_call(
        _merge_kernel,
        grid=(n // tm,),
        in_specs=[pl.BlockSpec((tm, d), row), half, half, half, half,
                  pl.BlockSpec((tm, d), lambda i: (i, 0)), pl.BlockSpec((tm, d), lambda i: (i, 1)),
                  _const_spec((1, RW_WIDTH)), _const_spec((1, RW_WIDTH)), _const_spec((RW_WIDTH, RW_WIDTH)),
                  _const_spec((RW_WIDTH, d)), _const_spec((DA_V_WIDTH, d)), _const_spec((d, d))],
        out_specs=pl.BlockSpec((tm, d), row),
        out_shape=jax.ShapeDtypeStruct((n, d), F32),
        compiler_params=_params(("parallel",)),
        name="merge",
    )(x, ys, bonus, g, yb, gates, gates, p["ln_w"], p["ln_b"], p["ones"], p["wpa"], p["wpb"], p["wout"])


def _flash_kernel(lam_ref, q_ref, k_ref, v_ref, sub_ref, o_ref, m_scr, l_scr, acc_scr, *, tq, out_scale):
    qi = pl.program_id(2)
    q = q_ref[...]
    lane = lax.broadcasted_iota(jnp.int32, q.shape, 1)
    zero = jnp.zeros_like(q)
    q2 = jnp.concatenate([jnp.where(lane < DA_QK, q, zero), jnp.where(lane >= DA_QK, q, zero)], axis=0)
    m_scr[...] = jnp.full_like(m_scr, -jnp.inf)
    l_scr[...] = jnp.zeros_like(l_scr)
    acc_scr[...] = jnp.zeros_like(acc_scr)
    nt = (((1,), (1,)), ((), ()))
    reps = tq // LANES

    def block(j, diagonal):
        off = pl.multiple_of(j * tq, tq)
        kb = k_ref[pl.ds(off, tq), :]
        vb = v_ref[pl.ds(off, tq), :]
        s = lax.dot_general(q2, kb, nt, preferred_element_type=F32)
        if diagonal:
            rr = lax.broadcasted_iota(jnp.int32, s.shape, 0) & (tq - 1)
            cc = lax.broadcasted_iota(jnp.int32, s.shape, 1)
            s = jnp.where(cc <= rr, s, -jnp.inf)
        m_old = m_scr[...]
        m_new = jnp.maximum(m_old, jnp.max(s, axis=-1, keepdims=True))
        alpha = jnp.exp2(m_old - m_new)
        p = jnp.exp2(s - jnp.concatenate([m_new] * reps, axis=1))
        l_scr[...] = alpha * l_scr[...] + jnp.sum(p, axis=-1, keepdims=True)
        acc_scr[...] = alpha * acc_scr[...] + _dot(p.astype(BF16), vb)
        m_scr[...] = m_new

    def body(jj, carry):
        block(2 * jj, False)
        block(2 * jj + 1, False)
        return carry

    lax.fori_loop(0, qi // 2, body, 0)

    @pl.when(qi % 2 == 1)
    def _():
        block(qi - 1, False)

    block(qi, True)

    o = acc_scr[...] / l_scr[...]
    o = o[:tq] - lam_ref[...] * o[tq:]
    o_ref[...] = _rms(o, sub_ref[...]) * out_scale


def _flash(lam_row, qb, kb, vb, subln, bsz, s, out_scale, tq=512):
    tq = min(tq, s)
    q3 = qb.reshape(bsz, s, DA_QK_WIDTH)
    k3 = kb.reshape(bsz, s, DA_QK_WIDTH)
    v3 = vb.reshape(bsz, s, DA_V_WIDTH)
    out = pl.pallas_call(
        functools.partial(_flash_kernel, tq=tq, out_scale=out_scale),
        grid=(bsz, DA_HEADS, s // tq),
        in_specs=[_const_spec((1, LANES)),
                  pl.BlockSpec((None, tq, LANES), lambda b, h, i: (b, i, h)),
                  pl.BlockSpec((None, s, LANES), lambda b, h, i: (b, 0, h)),
                  pl.BlockSpec((None, s, DA_V), lambda b, h, i: (b, 0, h)),
                  pl.BlockSpec((1, DA_V), lambda b, h, i: (0, h))],
        out_specs=pl.BlockSpec((None, tq, DA_V), lambda b, h, i: (b, i, h)),
        out_shape=jax.ShapeDtypeStruct((bsz, s, DA_V_WIDTH), F32),
        scratch_shapes=[pltpu.VMEM((2 * tq, LANES), F32), pltpu.VMEM((2 * tq, LANES), F32),
                        pltpu.VMEM((2 * tq, DA_V), F32)],
        compiler_params=_params(("parallel", "parallel", "arbitrary")),
        name="flash_diff",
    )(lam_row, q3, k3, v3, subln)
    return out.reshape(bsz * s, DA_V_WIDTH)


def _decode_kernel(pt_ref, lam_ref, q_ref, qd_ref, kn_ref, vn_ref, sub_ref, *rest, pages, out_scale):
    k_refs = rest[:pages]
    v_refs = rest[pages:2 * pages]
    o_ref, m_scr, l_scr, acc_scr = rest[2 * pages:]
    j = pl.program_id(1)
    own_head = lax.broadcasted_iota(jnp.int32, acc_scr.shape, 0) >> 1

    @pl.when(j == 0)
    def _():
        m_scr[...] = jnp.sum(q_ref[...] * kn_ref[...], axis=-1, keepdims=True)
        l_scr[...] = jnp.ones_like(l_scr)
        acc_scr[...] = vn_ref[...]

    qd = qd_ref[...]
    s = jnp.concatenate([_dot(qd, k_refs[u][...].astype(BF16)) for u in range(pages)], axis=1)
    m_old = m_scr[...]
    m_new = jnp.maximum(m_old, jnp.max(s, axis=-1, keepdims=True))
    alpha = jnp.exp2(m_old - m_new)
    p = jnp.exp2(s - m_new)
    l_scr[...] = alpha * l_scr[...] + jnp.sum(p, axis=-1, keepdims=True)
    m_scr[...] = m_new
    pv = jnp.zeros(acc_scr.shape, F32)
    for u in range(pages):
        pu = p[:, u * PAGE_SIZE:(u + 1) * PAGE_SIZE].astype(BF16)
        for hv in range(DA_HEADS):
            vh = v_refs[u][pl.ds(hv, PAGE_SIZE, stride=DA_HEADS), :].astype(BF16)
            pv = pv + jnp.where(own_head == hv, _dot(pu, vh), 0.0)
    acc_scr[...] = alpha * acc_scr[...] + pv

    @pl.when(j == pl.num_programs(1) - 1)
    def _():
        o8 = acc_scr[...] / l_scr[...]
        o = o8 - lam_ref[...] * pltpu.roll(o8, 2 * DA_HEADS - 1, 0)
        o_ref[...] = _rms(o, sub_ref[...]) * out_scale


def _decode_attn(lam_row, q8, k_new8, v_new8, subln8, cache_kt, cache_v, page_table, out_scale, pages=16):
    bsz, n_pages = page_table.shape
    pages = min(pages, n_pages)
    pt = page_table.reshape(-1)
    n_heads = 2 * DA_HEADS

    def page_map(u):
        return lambda b, j, pt_ref: (pt_ref[b * n_pages + j * pages + u], 0, 0)

    q_diag = (q8[:, :, None, :] * jnp.eye(n_heads, dtype=F32)[None, :, :, None]).reshape(bsz, n_heads, -1)
    seq = lambda b, j, pt_ref: (b, 0, 0)
    fixed = lambda b, j, pt_ref: (0, 0)
    in_specs = [pl.BlockSpec((1, LANES), fixed),
                pl.BlockSpec((None, n_heads, DA_QK), seq),
                pl.BlockSpec((None, n_heads, n_heads * DA_QK), seq),
                pl.BlockSpec((None, n_heads, DA_QK), seq),
                pl.BlockSpec((None, n_heads, DA_V), seq),
                pl.BlockSpec((n_heads, DA_V), fixed)]
    in_specs += [pl.BlockSpec((None, n_heads * DA_QK, PAGE_SIZE), page_map(u)) for u in range(pages)]
    in_specs += [pl.BlockSpec((None, PAGE_SIZE * DA_HEADS, DA_V), page_map(u)) for u in range(pages)]
    grid_spec = pltpu.PrefetchScalarGridSpec(
        num_scalar_prefetch=1,
        grid=(bsz, n_pages // pages),
        in_specs=in_specs,
        out_specs=pl.BlockSpec((None, n_heads, DA_V), seq),
        scratch_shapes=[pltpu.VMEM((n_heads, 1), F32), pltpu.VMEM((n_heads, 1), F32),
                        pltpu.VMEM((n_heads, DA_V), F32)])
    return pl.pallas_call(
        functools.partial(_decode_kernel, pages=pages, out_scale=out_scale),
        grid_spec=grid_spec,
        out_shape=jax.ShapeDtypeStruct((bsz, n_heads, DA_V), F32),
        compiler_params=_params(("parallel", "arbitrary")),
        name="decode_attn",
    )(pt, lam_row, q8, q_diag.astype(BF16), k_new8, v_new8, subln8,
      *([cache_kt] * pages), *([cache_v] * pages))


def _rope_tables(pos):
    half = DA_QK // 2
    inv = ROPE_THETA ** (-jnp.arange(half, dtype=F32) / half)
    ang = pos.astype(F32)[:, None] * inv[None, :]
    cos, sin = jnp.cos(ang), jnp.sin(ang)
    reps = LANES // DA_QK
    return jnp.tile(jnp.concatenate([cos, cos], -1), (1, reps)), jnp.tile(jnp.concatenate([-sin, sin], -1), (1, reps))


def _layer_params(l, ffn1_norm, ffn1_up, ffn1_down, mix_norm, w_in, gate_bias, rw_mu, rw_w0, rw_w2, rw_a0,
                  rw_a2, rw_g2, rw_k_k, rw_k_a, rw_r_k, rw_ln_w, rw_ln_b, da_subln, w_proj_a, w_proj_b, w_out,
                  ffn2_norm, ffn2_up, ffn2_down):
    def ffn_w(up, down):
        d, two_ff = up.shape
        ff = two_ff // 2
        tf = 256
        nc = ff // tf
        split = lambda w: w.reshape(d, nc, tf).transpose(1, 0, 2).astype(BF16)
        return split(up[:, :ff]), split(up[:, ff:]), down.reshape(nc, tf, d).astype(BF16)

    row = lambda v: v.reshape(1, -1)
    head_id = jnp.arange(RW_WIDTH) // RW_HEAD
    zeros = jnp.zeros((DECAY_LORA, RW_WIDTH), F32)
    return dict(
        ffn1=(row(ffn1_norm[l]),) + ffn_w(ffn1_up[l], ffn1_down[l]),
        ffn2=(row(ffn2_norm[l]),) + ffn_w(ffn2_up[l], ffn2_down[l]),
        mix_norm=row(mix_norm[l]), w_in=w_in[l].astype(BF16), gate_bias=row(gate_bias[l]),
        mu=row(rw_mu[l]), w0=row(rw_w0[l]), a0=row(rw_a0[l]), k_k=row(rw_k_k[l]), k_a=row(rw_k_a[l]),
        r_k=row(rw_r_k[l]),
        w2=jnp.concatenate([rw_w2[l], zeros], 0).astype(BF16),
        a2=jnp.concatenate([zeros, rw_a2[l]], 0).astype(BF16),
        g2=rw_g2[l].astype(BF16),
        ones=(head_id[:, None] == head_id[None, :]).astype(BF16),
        ln_w=row(rw_ln_w[l]), ln_b=row(rw_ln_b[l]), subln=row(da_subln[l]),
        wpa=w_proj_a[l].astype(BF16), wpb=w_proj_b[l].astype(BF16), wout=w_out[l].astype(BF16))


def _trunk_front(x, p, cos_t, sin_t, z_last, bsz, t):
    x1 = _ffn(x, *p["ffn1"])
    z, qb, k, v, kb, vb, gates = _mixproj(x1, p["mix_norm"], p["w_in"], p["gate_bias"], cos_t, sin_t, bsz, t)
    z3 = z.reshape(bsz, t, RW_COLS)
    if bsz == SCAN_LANE_BATCH and t % SCAN_ILO == 0:
        tpack = SCAN_ILO
        ops, g, bonus = _rwprep_lanes(z3, z_last, p)
        ops = [u[None] for u in ops]
    else:
        assert t == 1
        tpack = 1
        r, dec, kmod, vv, an, bn, g, bonus = _rwprep(z, z_last, p)
        ops = [_to_copied_lanes(u, bsz, t) for u in (an, dec, bn, kmod, r, vv)]
    return x1, z3, qb, k, v, kb, vb, gates, (ops, tpack), g, bonus


def _rwkv_scan(rw, wkv0, bsz, t):
    ops, tpack = rw
    y, s_fin = _wkv_scan(*ops, _state_to_lanes(wkv0), t, tpack)
    return _from_value_lanes(y, bsz, t), _state_from_lanes(s_fin, bsz)


def kernel(x_prompt, x_sample, cache_k, cache_v, state_wkv, state_shift, page_table, ffn1_norm, ffn1_up, ffn1_down, mix_norm, w_in, gate_bias, rw_mu, rw_w0, rw_w2, rw_a0, rw_a2, rw_g2, rw_k_k, rw_k_a, rw_r_k, rw_ln_w, rw_ln_b, da_lq1, da_lk1, da_lq2, da_lk2, da_subln, w_proj_a, w_proj_b, w_out, ffn2_norm, ffn2_up, ffn2_down, final_norm):
    bp, sp, d = x_prompt.shape
    bs, ts, _ = x_sample.shape
    depth = ffn1_norm.shape[0]
    assert ts == 1 and bp % SCAN_LANE_BATCH == 0 and bs % SCAN_LANE_BATCH == 0
    assert cache_k.shape[2] == PAGE_SIZE == LANES
    n_past = page_table.shape[1] * PAGE_SIZE
    cos_p, sin_p = _rope_tables(jnp.arange(sp, dtype=jnp.int32))
    cos_s, sin_s = _rope_tables(jnp.full((bs,), n_past, dtype=jnp.int32))
    final_row = final_norm.reshape(1, d)

    hp = x_prompt.reshape(bp * sp, d)
    hs = x_sample.reshape(bs * ts, d)
    outs = [[] for _ in range(8)]
    for l in range(depth):
        p = _layer_params(l, ffn1_norm, ffn1_up, ffn1_down, mix_norm, w_in, gate_bias, rw_mu, rw_w0, rw_w2,
                          rw_a0, rw_a2, rw_g2, rw_k_k, rw_k_a, rw_r_k, rw_ln_w, rw_ln_b, da_subln, w_proj_a,
                          w_proj_b, w_out, ffn2_norm, ffn2_up, ffn2_down)
        lam_init = 0.8 - 0.6 * math.exp(-0.3 * l)
        lam = (jnp.exp(jnp.sum(da_lq1[l] * da_lk1[l])) - jnp.exp(jnp.sum(da_lq2[l] * da_lk2[l])) + lam_init)
        lam_row = jnp.full((1, LANES), lam, F32)
        out_scale = 1.0 - lam_init
        last = l == depth - 1

        x1, z3, qb, k, v, kb, vb, gates, rw, g, bonus = _trunk_front(
            hp, p, cos_p, sin_p, jnp.zeros((bp, RW_COLS), F32), bp, sp)
        ys, wkv_p = _rwkv_scan(rw, jnp.zeros((bp, RW_HEADS, RW_HEAD, RW_HEAD), F32), bp, sp)
        yb = _flash(lam_row, qb, kb, vb, p["subln"], bp, sp, out_scale)
        x2 = _merge(x1, ys, bonus, g, yb, gates, p)
        hp = _ffn(x2, *p["ffn2"], final_g=final_row if last else None)
        outs[0].append(k)
        outs[1].append(v)
        outs[2].append(wkv_p)
        outs[3].append(z3[:, -1])

        x1, z3, qb, k, v, kb, vb, gates, rw, g, bonus = _trunk_front(
            hs, p, cos_s, sin_s, state_shift[l], bs, ts)
        ys, wkv_s = _rwkv_scan(rw, state_wkv[l], bs, ts)
        q8 = qb.astype(F32).reshape(bs, 2 * DA_HEADS, DA_QK)
        k8 = k.reshape(bs, 2 * DA_HEADS, DA_QK)
        v8 = jnp.repeat(v.reshape(bs, DA_HEADS, DA_V), 2, axis=1)
        sub8 = jnp.repeat(p["subln"].reshape(DA_HEADS, DA_V), 2, axis=0)
        cache_vf = cache_v[l].reshape(cache_v.shape[1], PAGE_SIZE * DA_HEADS, DA_V)
        cache_kt = jnp.transpose(cache_k[l], (0, 2, 3, 1)).reshape(cache_k.shape[1], DA_QK_WIDTH, PAGE_SIZE)
        o8 = _decode_attn(lam_row, q8, k8, v8, sub8, cache_kt, cache_vf, page_table, out_scale)
        yb = o8[:, ::2].reshape(bs, DA_V_WIDTH)
        x2 = _merge(x1, ys, bonus, g, yb, gates, p)
        hs = _ffn(x2, *p["ffn2"], final_g=final_row if last else None)
        outs[4].append(k)
        outs[5].append(v)
        outs[6].append(wkv_s)
        outs[7].append(z3[:, -1])

    st = [jnp.stack(o) for o in outs]
    return (hp.reshape(bp, sp, d), hs.reshape(bs, ts, d), st[0], st[1], st[2], st[3], st[4], st[5], st[6], st[7])
```
